```python
import jax, jax.numpy as jnp
from jax import lax
import numpy as np

D_MODEL = 1024
BATCH = 8
SEQ = 4096
DEPTH = 2

CTX_LEN = 256
GRID_W = 64
NORM_EPS = 1e-6

GLA_HEADS = 4
GLA_DK = 64
GLA_DV = 128
GLA_RANK = 16
GLA_TAU = 16.0
GLA_CHUNK = 64
GLA_QK = GLA_HEADS * GLA_DK
GLA_V = GLA_HEADS * GLA_DV

ATT_HEADS = 8
ATT_KV_HEADS = 2
ATT_DH = 64
ATT_BLOCK = 128
ROPE_THETA = 10000.0
ROPE_AXIS_DIM = ATT_DH // 2
ATT_Q = ATT_HEADS * ATT_DH
ATT_KV = ATT_KV_HEADS * ATT_DH

POOL_WINDOWS = (2, 4, 8, 16)
POOL_GROUP = 128
POOL_WIDTH = POOL_GROUP * len(POOL_WINDOWS)

N_BRANCH = 3
BRANCH_WIDTH = 512

N_GROUPS = 4
EXP_PER_GROUP = 8
N_EXPERTS = N_GROUPS * EXP_PER_GROUP
TOP_K = 2
D_EXPERT = 512
MOE_BLOCK = 128

IN_SIZES = (GLA_QK, GLA_QK, GLA_V, GLA_V, GLA_RANK, GLA_RANK, ATT_Q, ATT_KV, ATT_KV, POOL_WIDTH, N_BRANCH * D_MODEL)
IN_WIDTH = int(sum(IN_SIZES))
IN_SPLITS = tuple(int(s) for s in np.cumsum(IN_SIZES)[:-1])

kernel_name = 'hybrid_gla_gqa_pool_hmoe_dit'

F32 = jnp.float32


def rmsnorm(x, g):
    xf = x.astype(F32)
    y = xf * lax.rsqrt(jnp.mean(xf * xf, axis=-1, keepdims=True) + NORM_EPS)
    return (y * g.astype(F32)).astype(x.dtype)


def modulate(h, shift, scale):
    return h * (1 + scale[:, None]) + shift[:, None]


def axial_rope_angles(rows):
    row = jnp.repeat(jnp.arange(rows), GRID_W).astype(F32)
    col = jnp.tile(jnp.arange(GRID_W), rows).astype(F32)
    inv = ROPE_THETA ** (-jnp.arange(0, ROPE_AXIS_DIM, 2, dtype=F32) / ROPE_AXIS_DIM)
    return row[:, None] * inv, col[:, None] * inv


def _rotate(x, ang):
    x1, x2 = jnp.split(x, 2, axis=-1)
    cos = jnp.cos(ang)[None, :, None, :]
    sin = jnp.sin(ang)[None, :, None, :]
    return jnp.concatenate([x1 * cos - x2 * sin, x1 * sin + x2 * cos], axis=-1)


def rope_2d(x, ang_row, ang_col):
    xf = x.astype(F32)
    out = jnp.concatenate([_rotate(xf[..., :ROPE_AXIS_DIM], ang_row),
                           _rotate(xf[..., ROPE_AXIS_DIM:], ang_col)], axis=-1)
    return out.astype(x.dtype)


def gla_direction(q, k, v, log_a, s0):
    B, T, H, DK = k.shape
    DV = v.shape[-1]
    L = GLA_CHUNK
    n = T // L
    kc = k.astype(F32).reshape(B, n, L, H, DK)
    vc = v.astype(F32).reshape(B, n, L, H, DV)
    b = jnp.cumsum(log_a.astype(F32).reshape(B, n, L, H, DK), axis=2)
    b_last = b[:, :, -1:]
    d_state = jnp.einsum('bnshd,bnshv->bnhdv', kc * jnp.exp(b_last - b), vc)
    a_chunk = jnp.exp(b_last[:, :, 0])
    emit = q is not None

    def step(s, inp):
        a, d = inp
        return a[..., None] * s + d, (s if emit else None)

    s_fin, s_start = lax.scan(step, s0, (jnp.moveaxis(a_chunk, 1, 0), jnp.moveaxis(d_state, 1, 0)))
    if not emit:
        return None, s_fin
    s_start = jnp.moveaxis(s_start, 0, 1)
    qc = q.astype(F32).reshape(B, n, L, H, DK)
    b_ref = b[:, :, L // 2:L // 2 + 1]
    scores = jnp.einsum('bnthd,bnshd->bnhts', qc * jnp.exp(b - b_ref), kc * jnp.exp(b_ref - b))
    lower = jnp.tril(jnp.ones((L, L), dtype=bool))
    scores = jnp.where(lower, scores, 0.0)
    o = (jnp.einsum('bnhts,bnshv->bnthv', scores, vc)
         + jnp.einsum('bnthd,bnhdv->bnthv', qc * jnp.exp(b), s_start))
    return o.reshape(B, T, H, DV).astype(v.dtype), s_fin


def gla_bidir(q, k, v, la_f, la_b, s0_f, s0_b, want_out):
    flip = lambda t: jnp.flip(t, axis=1)
    o_f, s_f = gla_direction(q if want_out else None, k, v, la_f, s0_f)
    o_b, s_b = gla_direction(flip(q) if want_out else None, flip(k), flip(v), flip(la_b), s0_b)
    o = (o_f + flip(o_b)) if want_out else None
    return o, s_f, s_b


def gla_inputs(gq, gk, gv, glf, glb, up_f, bias_f, up_b, bias_b):
    B, T, _ = gq.shape
    shp = (B, T, GLA_HEADS, GLA_DK)
    q = gq.reshape(shp) * GLA_DK ** -0.5
    k = gk.reshape(shp)
    v = gv.reshape(B, T, GLA_HEADS, GLA_DV)
    la_f = (jax.nn.log_sigmoid((glf @ up_f + bias_f).astype(F32)) / GLA_TAU).reshape(shp)
    la_b = (jax.nn.log_sigmoid((glb @ up_b + bias_b).astype(F32)) / GLA_TAU).reshape(shp)
    return q, k, v, la_f, la_b


def gla_output(o, r, norm_g):
    B, T, H, DV = o.shape
    return rmsnorm(o, norm_g).reshape(B, T, H * DV) * jax.nn.silu(r)


def gqa_blocks(q, k, v):
    B, T, H, DH = q.shape
    HKV = k.shape[2]
    G = H // HKV
    nb = T // ATT_BLOCK
    qb = jnp.moveaxis(q.reshape(B, nb, ATT_BLOCK, HKV, G, DH), 1, 0)

    def one(qblk):
        s = jnp.einsum('bqkgd,bskd->bkgqs', qblk, k).astype(F32) * DH ** -0.5
        p = jax.nn.softmax(s, axis=-1).astype(v.dtype)
        return jnp.einsum('bkgqs,bskd->bqkgd', p, v)

    o = lax.map(one, qb)
    return jnp.moveaxis(o, 0, 1).reshape(B, T, H * DH)


def pool_mix(u, pool_w, pool_scale):
    B, T, W = u.shape
    uf = u.astype(F32)
    cs = jnp.concatenate([jnp.zeros((B, 1, W), F32), jnp.cumsum(uf, axis=1)], axis=1)
    t = jnp.arange(T)
    diffs = []
    for gi, win in enumerate(POOL_WINDOWS):
        sl = slice(gi * POOL_GROUP, (gi + 1) * POOL_GROUP)
        lo = jnp.clip(t - win // 2, 0, T)
        hi = jnp.clip(t + win // 2, 0, T)
        csg = cs[..., sl]
        mean = (csg[:, hi] - csg[:, lo]) / (hi - lo).astype(F32)[None, :, None]
        diffs.append(mean - uf[..., sl])
    d = jnp.stack(diffs, axis=2)
    y = jnp.einsum('btgc,gce->btge', d, pool_w.astype(F32)).reshape(B, T, W) * pool_scale.astype(F32)
    return y.astype(u.dtype)


def branch_merge(y_gla, y_att, y_pool, gt, w_branch, w_out):
    g = jax.nn.sigmoid(gt.reshape(gt.shape[:-1] + (N_BRANCH, -1)))
    z = (g[..., 0, :] * (y_gla @ w_branch[0])
         + g[..., 1, :] * (y_att @ w_branch[1])
         + g[..., 2, :] * (y_pool @ w_branch[2]))
    return z @ w_out


def mixing(h, hc, ang_row, ang_col, w_in, gla_a_up_f, gla_a_bias_f, gla_a_up_b, gla_a_bias_b,
           gla_norm_g, att_qn_g, att_kn_g, pool_w, pool_scale, w_branch, w_out, want_ctx):
    B, T, _ = h.shape
    C = hc.shape[1]
    (gq, gk, gv, gr, glf, glb, aq, ak, av, pu, gt) = jnp.split(h @ w_in, IN_SPLITS, axis=-1)
    (gq_c, gk_c, gv_c, gr_c, glf_c, glb_c, aq_c, ak_c, av_c, pu_c, gt_c) = jnp.split(hc @ w_in, IN_SPLITS, axis=-1)
    q_c, k_c, v_c, laf_c, lab_c = gla_inputs(gq_c, gk_c, gv_c, glf_c, glb_c, gla_a_up_f, gla_a_bias_f, gla_a_up_b, gla_a_bias_b)
    s0 = jnp.zeros((B, GLA_HEADS, GLA_DK, GLA_DV), F32)
    o_c, sf_c, sb_c = gla_bidir(q_c, k_c, v_c, laf_c, lab_c, s0, s0, want_ctx)
    q, k, v, laf, lab = gla_inputs(gq, gk, gv, glf, glb, gla_a_up_f, gla_a_bias_f, gla_a_up_b, gla_a_bias_b)
    o, _, _ = gla_bidir(q, k, v, laf, lab, sf_c, sb_c, True)
    y_gla = gla_output(o, gr, gla_norm_g)
    ka_c = rmsnorm(ak_c.reshape(B, C, ATT_KV_HEADS, ATT_DH), att_kn_g)
    va_c = av_c.reshape(B, C, ATT_KV_HEADS, ATT_DH)
    qa = rope_2d(rmsnorm(aq.reshape(B, T, ATT_HEADS, ATT_DH), att_qn_g), ang_row, ang_col)
    ka = rope_2d(rmsnorm(ak.reshape(B, T, ATT_KV_HEADS, ATT_DH), att_kn_g), ang_row, ang_col)
    va = av.reshape(B, T, ATT_KV_HEADS, ATT_DH)
    y_att = gqa_blocks(qa, jnp.concatenate([ka, ka_c], axis=1), jnp.concatenate([va, va_c], axis=1))
    y_pool = pool_mix(pu, pool_w, pool_scale)
    y = branch_merge(y_gla, y_att, y_pool, gt, w_branch, w_out)
    if not want_ctx:
        return y, None
    qa_c = rmsnorm(aq_c.reshape(B, C, ATT_HEADS, ATT_DH), att_qn_g)
    yc = branch_merge(gla_output(o_c, gr_c, gla_norm_g), gqa_blocks(qa_c, ka_c, va_c),
                      pool_mix(pu_c, pool_w, pool_scale), gt_c, w_branch, w_out)
    return y, yc


def routed_ffn(h, w_group, b_group, w_expert, b_expert, w_gate, w_up, w_down):
    B, S, D = h.shape
    N = B * S
    xt = h.reshape(N, D)
    p_group = jax.nn.softmax((xt @ w_group).astype(F32) + b_group, axis=-1)
    p_top_group, grp = lax.top_k(p_group, 1)
    grp = grp[:, 0]
    logits_e = ((xt @ w_expert).astype(F32) + b_expert).reshape(N, N_GROUPS, EXP_PER_GROUP)
    p_in = jax.nn.softmax(logits_e[jnp.arange(N), grp], axis=-1)
    p_top, idx = lax.top_k(p_in, TOP_K)
    wts = (p_top_group * p_top / jnp.sum(p_top, axis=-1, keepdims=True)).reshape(-1)
    eid = (grp[:, None] * EXP_PER_GROUP + idx).reshape(-1)
    M = N * TOP_K
    tok = jnp.arange(M) // TOP_K
    order = jnp.argsort(eid)
    e_s, tok_s, w_s = eid[order], tok[order], wts[order]
    counts = jnp.bincount(eid, length=N_EXPERTS)
    padded = (counts + MOE_BLOCK - 1) // MOE_BLOCK * MOE_BLOCK
    seg_start = jnp.cumsum(counts) - counts
    pad_end = jnp.cumsum(padded)
    pad_start = pad_end - padded
    dest = pad_start[e_s] + jnp.arange(M) - seg_start[e_s]
    n_blocks = -(-(M + N_EXPERTS * (MOE_BLOCK - 1)) // MOE_BLOCK)
    P = n_blocks * MOE_BLOCK
    buf_tok = jnp.full((P,), N, jnp.int32).at[dest].set(tok_s)
    buf_w = jnp.zeros((P,), F32).at[dest].set(w_s)
    blk_expert = jnp.minimum(jnp.searchsorted(pad_end, jnp.arange(n_blocks) * MOE_BLOCK, side='right'), N_EXPERTS - 1)
    x_pad = jnp.concatenate([xt, jnp.zeros((1, D), xt.dtype)], axis=0)

    def expert_block(args):
        tb, e = args
        xb = x_pad[tb]
        return (jax.nn.silu(xb @ w_gate[e]) * (xb @ w_up[e])) @ w_down[e]

    yb = lax.map(expert_block, (buf_tok.reshape(n_blocks, MOE_BLOCK), blk_expert))
    yb = yb.reshape(P, D) * buf_w[:, None].astype(yb.dtype)
    out = jnp.zeros((N + 1, D), yb.dtype).at[buf_tok].add(yb)[:N]
    return out.reshape(B, S, D).astype(h.dtype)


def setup_inputs(seed: int = 0) -> dict:
    key = jax.random.key(seed)
    ks = jax.random.split(key, 32)
    L = DEPTH
    D = D_MODEL
    nrm = lambda k, shape, s: jax.random.normal(k, shape, F32) * s
    return {
        'x': nrm(ks[0], (BATCH, SEQ, D), 1.0),
        'c': nrm(ks[1], (BATCH, D), 1.0),
        'ctx': nrm(ks[2], (BATCH, CTX_LEN, D), 1.0),
        'c_ctx': nrm(ks[3], (D,), 1.0),
        'w_mod': nrm(ks[4], (L, D, 6 * D), 0.5 * D ** -0.5),
        'b_mod': nrm(ks[5], (L, 6 * D), 0.02),
        'norm1_g': 1.0 + nrm(ks[6], (L, D), 0.02),
        'norm2_g': 1.0 + nrm(ks[7], (L, D), 0.02),
        'w_in': nrm(ks[8], (L, D, IN_WIDTH), D ** -0.5),
        'gla_a_up_f': nrm(ks[9], (L, GLA_RANK, GLA_QK), GLA_RANK ** -0.5),
        'gla_a_bias_f': nrm(ks[10], (L, GLA_QK), 0.1),
        'gla_a_up_b': nrm(ks[11], (L, GLA_RANK, GLA_QK), GLA_RANK ** -0.5),
        'gla_a_bias_b': nrm(ks[12], (L, GLA_QK), 0.1),
        'gla_norm_g': 1.0 + nrm(ks[13], (L, GLA_DV), 0.02),
        'att_qn_g': 1.0 + nrm(ks[14], (L, ATT_DH), 0.02),
        'att_kn_g': 1.0 + nrm(ks[15], (L, ATT_DH), 0.02),
        'pool_w': nrm(ks[16], (L, len(POOL_WINDOWS), POOL_GROUP, POOL_GROUP), POOL_GROUP ** -0.5),
        'pool_scale': 1.0 + nrm(ks[17], (L, POOL_WIDTH), 0.02),
        'w_branch': nrm(ks[18], (L, N_BRANCH, BRANCH_WIDTH, D), BRANCH_WIDTH ** -0.5),
        'w_out': nrm(ks[19], (L, D, D), D ** -0.5),
        'moe_w_group': nrm(ks[20], (L, D, N_GROUPS), D ** -0.5),
        'moe_b_group': nrm(ks[21], (L, N_GROUPS), 0.01),
        'moe_w_expert': nrm(ks[22], (L, D, N_EXPERTS), D ** -0.5),
        'moe_b_expert': nrm(ks[23], (L, N_EXPERTS), 0.01),
        'moe_w_gate': nrm(ks[24], (L, N_EXPERTS, D, D_EXPERT), D ** -0.5),
        'moe_w_up': nrm(ks[25], (L, N_EXPERTS, D, D_EXPERT), D ** -0.5),
        'moe_w_down': nrm(ks[26], (L, N_EXPERTS, D_EXPERT, D), D_EXPERT ** -0.5),
        'final_g': 1.0 + nrm(ks[27], (D,), 0.02),
    }


def reference(x, c, ctx, c_ctx, w_mod, b_mod, norm1_g, norm2_g, w_in, gla_a_up_f, gla_a_bias_f,
              gla_a_up_b, gla_a_bias_b, gla_norm_g, att_qn_g, att_kn_g, pool_w, pool_scale,
              w_branch, w_out, moe_w_group, moe_b_group, moe_w_expert, moe_b_expert,
              moe_w_gate, moe_w_up, moe_w_down, final_g):
    B, T, D = x.shape
    rows = T // GRID_W
    ang_row, ang_col = axial_rope_angles(rows)
    xc = ctx
    s_lat = jax.nn.silu(c)
    s_ctx = jax.nn.silu(c_ctx)[None]
    for l in range(DEPTH):
        want_ctx = l < DEPTH - 1
        sh1, sc1, g1, sh2, sc2, g2 = jnp.split(s_lat @ w_mod[l] + b_mod[l], 6, axis=-1)
        csh1, csc1, cg1, csh2, csc2, cg2 = jnp.split(s_ctx @ w_mod[l] + b_mod[l], 6, axis=-1)
        h = modulate(rmsnorm(x, norm1_g[l]), sh1, sc1)
        hc = modulate(rmsnorm(xc, norm1_g[l]), csh1, csc1)
        y, yc = mixing(h, hc, ang_row, ang_col, w_in[l], gla_a_up_f[l], gla_a_bias_f[l],
                       gla_a_up_b[l], gla_a_bias_b[l], gla_norm_g[l], att_qn_g[l], att_kn_g[l],
                       pool_w[l], pool_scale[l], w_branch[l], w_out[l], want_ctx)
        x = x + g1[:, None] * y
        h2 = modulate(rmsnorm(x, norm2_g[l]), sh2, sc2)
        moe_par = (moe_w_group[l], moe_b_group[l], moe_w_expert[l], moe_b_expert[l],
                   moe_w_gate[l], moe_w_up[l], moe_w_down[l])
        if want_ctx:
            xc = xc + cg1[:, None] * yc
            h2c = modulate(rmsnorm(xc, norm2_g[l]), csh2, csc2)
            m = routed_ffn(jnp.concatenate([h2, h2c], axis=1), *moe_par)
            x = x + g2[:, None] * m[:, :T]
            xc = xc + cg2[:, None] * m[:, T:]
        else:
            x = x + g2[:, None] * routed_ffn(h2, *moe_par)
    return rmsnorm(x, final_g)
```

```python
import functools

import numpy as np
import jax
import jax.numpy as jnp
from jax import lax
from jax.experimental import pallas as pl
from jax.experimental.pallas import tpu as pltpu

F32 = jnp.float32
BF16 = jnp.bfloat16

VMEM_LIMIT_BYTES = 56 * 1024 * 1024
LANES = 128

D_MODEL = 1024
GRID_W = 64
NORM_EPS = 1e-6
GLA_HEADS, GLA_DK, GLA_DV, GLA_RANK, GLA_TAU, GLA_CHUNK = 4, 64, 128, 16, 16.0, 64
GLA_QK, GLA_V = GLA_HEADS * GLA_DK, GLA_HEADS * GLA_DV
ATT_HEADS, ATT_KV_HEADS, ATT_DH = 8, 2, 64
ROPE_THETA, ROPE_AXIS_DIM = 10000.0, 32
ATT_Q, ATT_KV = ATT_HEADS * ATT_DH, ATT_KV_HEADS * ATT_DH
POOL_WINDOWS, POOL_GROUP = (2, 4, 8, 16), 128
POOL_WIDTH = POOL_GROUP * len(POOL_WINDOWS)
POOL_HALO = 16
N_BRANCH, BRANCH_WIDTH = 3, 512
N_GROUPS, EXP_PER_GROUP, TOP_K, D_EXPERT = 4, 8, 2, 512
N_EXPERTS = N_GROUPS * EXP_PER_GROUP
IN_SIZES = (GLA_QK, GLA_QK, GLA_V, GLA_V, GLA_RANK, GLA_RANK, ATT_Q, ATT_KV, ATT_KV, POOL_WIDTH, N_BRANCH * D_MODEL)

OFF_GT, OFF_GQK, OFF_GV, OFF_GR, OFF_AQ, OFF_PU, OFF_AKV, OFF_GL = 0, 3072, 3584, 4096, 4608, 5120, 5632, 5888
U_WIDTH = 6144
U_CHUNK = 512

TM_IN = 512
TB_GLA = 256
TQ_ATT = 512
TK_ATT = 512
TM_MERGE = 512
BM_MOE = 256
TM_COMB = 512
ROUTER_W = 128


def _cparams(sem):
    return pltpu.CompilerParams(dimension_semantics=sem, vmem_limit_bytes=VMEM_LIMIT_BYTES)


def _split_bf16(a):
    hi = a.astype(BF16)
    lo = (a - hi.astype(F32)).astype(BF16)
    return hi, lo


def _dot(a, b):
    return jnp.dot(a, b, preferred_element_type=F32)


def _dot_nt(a, b):
    return lax.dot_general(a, b, (((1,), (1,)), ((), ())), preferred_element_type=F32)


def _dot_tn(a, b):
    return lax.dot_general(a, b, (((0,), (0,)), ((), ())), preferred_element_type=F32)


def _dot3(a, b):
    ah, al = _split_bf16(a)
    bh, bl = _split_bf16(b)
    return _dot(ah, bh) + _dot(ah, bl) + _dot(al, bh)


def _sigmoid(x):
    return 1.0 / (1.0 + jnp.exp(-x))


def _silu(x):
    return x * _sigmoid(x)


def _mod_kernel(s_ref, w_ref, b_ref, o_ref):
    s = _silu(s_ref[...])
    o_ref[...] = _dot3(s, w_ref[...]) + b_ref[...]


def _mod_call(s_in, w_mod, b_mod):
    L, D, _ = w_mod.shape
    R = s_in.shape[0]
    return pl.pallas_call(
        _mod_kernel,
        grid=(L, 6),
        in_specs=[
            pl.BlockSpec((R, D), lambda l, j: (0, 0)),
            pl.BlockSpec((None, D, D), lambda l, j: (l, 0, j)),
            pl.BlockSpec((None, None, 1, D), lambda l, j: (l, j, 0, 0)),
        ],
        out_specs=pl.BlockSpec((None, None, R, D), lambda l, j: (l, j, 0, 0)),
        out_shape=jax.ShapeDtypeStruct((L, 6, R, D), F32),
        compiler_params=_cparams(("arbitrary", "arbitrary")),
        name="mod_table",
    )(s_in, w_mod, b_mod.reshape(L, 6, 1, D))


def _inproj_kernel(x_ref, sh_ref, sc_ref, g_ref, w_ref, o_ref, *, n_lat_tiles, tiles_per_batch, ctx_row):
    i = pl.program_id(0)
    b = jnp.where(i < n_lat_tiles, i // tiles_per_batch, ctx_row)
    x = x_ref[...]
    ms = jnp.mean(x * x, axis=-1, keepdims=True)
    h = x * lax.rsqrt(ms + NORM_EPS) * g_ref[...]
    h = h * (1.0 + sc_ref[pl.ds(b, 1), :]) + sh_ref[pl.ds(b, 1), :]
    hb = h.astype(BF16)
    for c in range(U_WIDTH // U_CHUNK):
        cs = slice(c * U_CHUNK, (c + 1) * U_CHUNK)
        o_ref[:, cs] = _dot(hb, w_ref[:, cs]).astype(BF16)


def _inproj_call(x, mods, l, norm_g, w_perm, n_lat_rows, rows_per_batch, ctx_row):
    N, D = x.shape
    R = mods.shape[2]
    tm = TM_IN
    kern = functools.partial(_inproj_kernel, n_lat_tiles=n_lat_rows // tm,
                             tiles_per_batch=rows_per_batch // tm, ctx_row=ctx_row)
    return pl.pallas_call(
        kern,
        grid=(N // tm,),
        in_specs=[
            pl.BlockSpec((tm, D), lambda i: (i, 0)),
            pl.BlockSpec((None, None, R, D), lambda i: (l, 0, 0, 0)),
            pl.BlockSpec((None, None, R, D), lambda i: (l, 1, 0, 0)),
            pl.BlockSpec((1, D), lambda i: (0, 0)),
            pl.BlockSpec((D, U_WIDTH), lambda i: (0, 0)),
        ],
        out_specs=pl.BlockSpec((tm, U_WIDTH), lambda i: (i, 0)),
        out_shape=jax.ShapeDtypeStruct((N, U_WIDTH), BF16),
        compiler_params=_cparams(("arbitrary",)),
        name="inproj",
    )(x, mods, mods, norm_g.reshape(1, D), w_perm)


def _gla_chunk(qk, v, gl, up, bias, st_ref, forward):
    Lc = GLA_CHUNK
    q = qk[:, :GLA_QK].astype(F32) * (GLA_DK ** -0.5)
    k = qk[:, GLA_QK:].astype(F32)
    z = _dot(gl, up) + bias
    la = (jnp.minimum(z, 0.0) - jnp.log(1.0 + jnp.exp(-jnp.abs(z)))) * (1.0 / GLA_TAU)
    r = lax.broadcasted_iota(jnp.int32, (Lc, Lc), 0)
    c = lax.broadcasted_iota(jnp.int32, (Lc, Lc), 1)
    tri = jnp.where((r >= c) if forward else (r <= c), 1.0, 0.0).astype(BF16)
    la_hi, la_lo = _split_bf16(la)
    b = _dot(tri, la_hi) + _dot(tri, la_lo)
    i_last, i_ref = (Lc - 1, Lc // 2) if forward else (0, Lc - 1 - Lc // 2)
    b_last = b[i_last:i_last + 1, :]
    b_ref = b[i_ref:i_ref + 1, :]
    qd = (q * jnp.exp(b - b_ref)).astype(BF16)
    kd = (k * jnp.exp(b_ref - b)).astype(BF16)
    qe = (q * jnp.exp(b)).astype(BF16)
    kl = (k * jnp.exp(b_last - b)).astype(BF16)
    a = jnp.exp(b_last)
    lane_head = lax.broadcasted_iota(jnp.int32, (Lc, GLA_QK), 1) // GLA_DK
    kd_blk = jnp.concatenate([jnp.where(lane_head == h, kd, jnp.zeros_like(kd)) for h in range(GLA_HEADS)], axis=0)
    sc = _dot_nt(qd, kd_blk)
    t_i = lax.broadcasted_iota(jnp.int32, (Lc, GLA_HEADS * Lc), 0)
    s_i = lax.broadcasted_iota(jnp.int32, (Lc, GLA_HEADS * Lc), 1) % Lc
    sc = jnp.where((s_i <= t_i) if forward else (s_i >= t_i), sc, 0.0).astype(BF16)
    vlane_head = lax.broadcasted_iota(jnp.int32, (Lc, GLA_V), 1) // GLA_DV
    v_blk = jnp.concatenate([jnp.where(vlane_head == h, v, jnp.zeros_like(v)) for h in range(GLA_HEADS)], axis=0)
    st = st_ref[...]
    o = _dot(sc, v_blk) + _dot_nt(qe, st.astype(BF16))
    ds = _dot_tn(v, kl)
    row_head = lax.broadcasted_iota(jnp.int32, (GLA_V, GLA_QK), 0) // GLA_DV
    col_head = lax.broadcasted_iota(jnp.int32, (GLA_V, GLA_QK), 1) // GLA_DK
    st_ref[...] = a * st + jnp.where(row_head == col_head, ds, 0.0)
    return o


def _gla_kernel(qkf_ref, vf_ref, glf_ref, qkb_ref, vb_ref, glb_ref, up_ref, bias_ref, s0f_ref, s0b_ref,
                of_ref, ob_ref, sf_ref, sb_ref, stf, stb, *, nsub):
    n = pl.program_id(1)

    @pl.when(n == 0)
    def _():
        stf[...] = s0f_ref[...]
        stb[...] = s0b_ref[...]

    def body(j, carry):
        rf = pl.ds(pl.multiple_of(j * GLA_CHUNK, GLA_CHUNK), GLA_CHUNK)
        of_ref[rf, :] = _gla_chunk(qkf_ref[rf, :], vf_ref[rf, :], glf_ref[rf, :], up_ref[:, :GLA_QK],
                                   bias_ref[:, :GLA_QK], stf, True)
        rb = pl.ds(pl.multiple_of((nsub - 1 - j) * GLA_CHUNK, GLA_CHUNK), GLA_CHUNK)
        ob_ref[rb, :] = _gla_chunk(qkb_ref[rb, :], vb_ref[rb, :], glb_ref[rb, :], up_ref[:, GLA_QK:],
                                   bias_ref[:, GLA_QK:], stb, False)
        return carry

    lax.fori_loop(0, nsub, body, 0)

    @pl.when(n == pl.num_programs(1) - 1)
    def _():
        sf_ref[...] = stf[...]
        sb_ref[...] = stb[...]


def _gla_call(u, row_off, B, T, s0f, s0b, up, bias):
    tb = min(TB_GLA, T)
    nb = T // tb
    off = row_off // tb
    fwd = lambda b, n: off + b * nb + n
    bwd = lambda b, n: off + b * nb + (nb - 1 - n)
    cqk, cv, cgl = OFF_GQK // 512, OFF_GV // 512, OFF_GL // 128
    st_spec = pl.BlockSpec((None, GLA_V, GLA_QK), lambda b, n: (b, 0, 0))
    st_shape = jax.ShapeDtypeStruct((B, GLA_V, GLA_QK), F32)
    o_shape = jax.ShapeDtypeStruct((B * T, GLA_V), F32)
    return pl.pallas_call(
        functools.partial(_gla_kernel, nsub=tb // GLA_CHUNK),
        grid=(B, nb),
        in_specs=[
            pl.BlockSpec((tb, 512), lambda b, n: (fwd(b, n), cqk)),
            pl.BlockSpec((tb, 512), lambda b, n: (fwd(b, n), cv)),
            pl.BlockSpec((tb, 128), lambda b, n: (fwd(b, n), cgl)),
            pl.BlockSpec((tb, 512), lambda b, n: (bwd(b, n), cqk)),
            pl.BlockSpec((tb, 512), lambda b, n: (bwd(b, n), cv)),
            pl.BlockSpec((tb, 128), lambda b, n: (bwd(b, n), cgl)),
            pl.BlockSpec((128, 2 * GLA_QK), lambda b, n: (0, 0)),
            pl.BlockSpec((1, 2 * GLA_QK), lambda b, n: (0, 0)),
            st_spec, st_spec,
        ],
        out_specs=[
            pl.BlockSpec((tb, GLA_V), lambda b, n: (b * nb + n, 0)),
            pl.BlockSpec((tb, GLA_V), lambda b, n: (b * nb + (nb - 1 - n), 0)),
            st_spec, st_spec,
        ],
        out_shape=[o_shape, o_shape, st_shape, st_shape],
        scratch_shapes=[pltpu.VMEM((GLA_V, GLA_QK), F32), pltpu.VMEM((GLA_V, GLA_QK), F32)],
        compiler_params=_cparams(("arbitrary", "arbitrary")),
        name="gla_scan",
    )(u, u, u, u, u, u, up, bias, s0f, s0b)


def _group_ms64(x):
    i = lax.broadcasted_iota(jnp.int32, (LANES, LANES), 0) // ATT_DH
    j = lax.broadcasted_iota(jnp.int32, (LANES, LANES), 1) // ATT_DH
    bd = jnp.where(i == j, 1.0, 0.0).astype(BF16)
    hi, lo = _split_bf16(x * x)
    return (_dot(hi, bd) + _dot(lo, bd)) * (1.0 / ATT_DH)


def _rope128(x, cos, s1, s2):
    return x * cos + pltpu.roll(x, LANES - ROPE_AXIS_DIM // 2, axis=1) * s1 + pltpu.roll(x, ROPE_AXIS_DIM // 2, axis=1) * s2


def _att_kernel(*refs, T, C, tk, use_rope):
    if T:
        (q_ref, kvl_ref, kvc_ref, cq_ref, s1q_ref, s2q_ref, ck_ref, s1k_ref, s2k_ref, qg_ref, kg_ref,
         o_ref, kx, vx) = refs
    else:
        q_ref, kvc_ref, qg_ref, kg_ref, o_ref, kx, vx = refs
    i = pl.program_id(1)
    lo_half = lax.broadcasted_iota(jnp.int32, (1, LANES), 1) < ATT_DH

    def put_kv(rows, kv, rope_tabs):
        k = kv[:, :LANES].astype(F32)
        v = kv[:, LANES:].astype(F32)
        k = k * lax.rsqrt(_group_ms64(k) + NORM_EPS) * kg_ref[...]
        if rope_tabs is not None:
            k = _rope128(k, *rope_tabs)
        k_sw = pltpu.roll(k, ATT_DH, axis=1)
        v_sw = pltpu.roll(v, ATT_DH, axis=1)
        for (a, a_sw, dst) in ((k, k_sw, kx), (v, v_sw, vx)):
            dst[0, rows, :] = jnp.where(lo_half, a, 0.0).astype(BF16)
            dst[1, rows, :] = jnp.where(lo_half, 0.0, a_sw).astype(BF16)
            dst[2, rows, :] = jnp.where(lo_half, a_sw, 0.0).astype(BF16)
            dst[3, rows, :] = jnp.where(lo_half, 0.0, a).astype(BF16)

    @pl.when(i == 0)
    def _():
        if T:
            def body(j, carry):
                rows = pl.ds(pl.multiple_of(j * tk, tk), tk)
                put_kv(rows, kvl_ref[rows, :], (ck_ref[rows, :], s1k_ref[rows, :], s2k_ref[rows, :]))
                return carry
            lax.fori_loop(0, T // tk, body, 0)
        put_kv(pl.ds(T, C), kvc_ref[...], None)

    n_slab = ATT_Q // LANES
    q_slabs = []
    for s in range(n_slab):
        q = q_ref[:, s * LANES:(s + 1) * LANES].astype(F32)
        q = q * lax.rsqrt(_group_ms64(q) + NORM_EPS) * qg_ref[...]
        if use_rope:
            q = _rope128(q, cq_ref[...], s1q_ref[...], s2q_ref[...])
        q_slabs.append((q * (ATT_DH ** -0.5)).astype(BF16))

    tq = q_ref.shape[0]
    heads_per_kv = ATT_HEADS // ATT_KV_HEADS
    for s in range(n_slab):
        o_slab = None
        for half in range(2):
            h = 2 * s + half
            var = (h // heads_per_kv) * 2 + half
            qs = q_slabs[s]

            def step(rows, carry):
                m, l, acc = carry
                sc = _dot_nt(qs, kx[var, rows, :])
                m_new = jnp.maximum(m, jnp.max(sc, axis=-1, keepdims=True))
                alpha = jnp.exp(m - m_new)
                p = jnp.exp(sc - m_new)
                l = alpha * l + jnp.sum(p, axis=-1, keepdims=True)
                acc = alpha * acc + _dot(p.astype(BF16), vx[var, rows, :])
                return m_new, l, acc

            carry = (jnp.full((tq, 1), -jnp.inf, F32), jnp.zeros((tq, 1), F32), jnp.zeros((tq, LANES), F32))
            if T:
                carry = lax.fori_loop(
                    0, T // tk, lambda j, cr: step(pl.ds(pl.multiple_of(j * tk, tk), tk), cr), carry)
            m, l, acc = step(pl.ds(T, C), carry)
            o_h = acc / l
            o_slab = o_h if o_slab is None else o_slab + o_h
        o_ref[:, s * LANES:(s + 1) * LANES] = o_slab.astype(BF16)


def _att_call(u, q_row_off, B, Tq, T, C, ctx_row_off, tabs, qg, kg):
    tq = min(TQ_ATT, Tq)
    nq = Tq // tq
    tk = TK_ATT
    S = T + C
    cq, ckv = OFF_AQ // 512, OFF_AKV // 256
    qoff = q_row_off // tq
    coff = ctx_row_off // C
    g_spec = pl.BlockSpec((1, LANES), lambda b, i: (0, 0))
    q_spec = pl.BlockSpec((tq, ATT_Q), lambda b, i: (qoff + b * nq + i, cq))
    kvc_spec = pl.BlockSpec((C, 2 * ATT_KV), lambda b, i: (coff + b, ckv))
    if T:
        tq_tab = pl.BlockSpec((tq, LANES), lambda b, i: (i, 0))
        tk_tab = pl.BlockSpec((T, LANES), lambda b, i: (0, 0))
        in_specs = [q_spec, pl.BlockSpec((T, 2 * ATT_KV), lambda b, i: (b, ckv)), kvc_spec,
                    tq_tab, tq_tab, tq_tab, tk_tab, tk_tab, tk_tab, g_spec, g_spec]
        args = (u, u, u, tabs[0], tabs[1], tabs[2], tabs[0], tabs[1], tabs[2], qg, kg)
    else:
        in_specs = [q_spec, kvc_spec, g_spec, g_spec]
        args = (u, u, qg, kg)
    return pl.pallas_call(
        functools.partial(_att_kernel, T=T, C=C, tk=tk, use_rope=bool(T)),
        grid=(B, nq),
        in_specs=in_specs,
        out_specs=pl.BlockSpec((tq, ATT_Q), lambda b, i: (b * nq + i, 0)),
        out_shape=jax.ShapeDtypeStruct((B * Tq, ATT_Q), BF16),
        scratch_shapes=[pltpu.VMEM((4, S, LANES), BF16), pltpu.VMEM((4, S, LANES), BF16)],
        compiler_params=_cparams(("arbitrary", "arbitrary")),
        name="gqa_lat" if T else "gqa_ctx",
    )(*args)


def _merge_kernel(*refs, n_seq_tiles, fill_tail, **kw):
    xo_ref, h2_ref, lg_ref = refs[-3:]
    if not fill_tail:
        _merge_body(*refs, **kw)
        return
    i = pl.program_id(0)

    @pl.when(i < n_seq_tiles)
    def _():
        _merge_body(*refs, **kw)

    @pl.when(i >= n_seq_tiles)
    def _():
        for r in (xo_ref, h2_ref, lg_ref):
            r[...] = jnp.zeros_like(r)


def _merge_body(*refs, T, tm, tiles_per_batch, row_base):
    (of_ref, ob_ref, gr_ref, pu_ref, pp_ref, pn_ref, gt_ref, ya_ref, x_ref, g1_ref, sh2_ref, sc2_ref,
     gng_ref, band_ref, pw_ref, ps_ref, wb_ref, wo_ref, n2g_ref, wrh_ref, wrl_ref, br_ref) = refs[:22]
    xo_ref, h2_ref, lg_ref = refs[-3:]
    i = pl.program_id(0)
    it = i % tiles_per_batch
    b = i // tiles_per_batch if row_base is None else row_base

    o = of_ref[...] + ob_ref[...]
    gr = gr_ref[...].astype(F32)
    parts = []
    for h in range(GLA_HEADS):
        hs = slice(h * GLA_DV, (h + 1) * GLA_DV)
        oh = o[:, hs]
        ms = jnp.mean(oh * oh, axis=-1, keepdims=True)
        parts.append(oh * lax.rsqrt(ms + NORM_EPS) * gng_ref[...])
    y_gla = (jnp.concatenate(parts, axis=1) * _silu(gr)).astype(BF16)

    prev = jnp.where(it > 0, pp_ref[...], jnp.zeros_like(pp_ref[...]))
    nxt = jnp.where(it < tiles_per_batch - 1, pn_ref[...], jnp.zeros_like(pn_ref[...]))
    ext = jnp.concatenate([prev, pu_ref[...], nxt], axis=0)
    sub = 128
    yp_rows = []
    for r in range(tm // sub):
        e = ext[r * sub:r * sub + sub + 2 * POOL_HALO, :]
        t = it * tm + r * sub + lax.broadcasted_iota(jnp.int32, (sub, 1), 0)
        cols = []
        for gi, win in enumerate(POOL_WINDOWS):
            gs = slice(gi * POOL_GROUP, (gi + 1) * POOL_GROUP)
            wsum = _dot(band_ref[gi], e[:, gs])
            cnt = (jnp.minimum(t + win // 2, T) - jnp.maximum(t - win // 2, 0)).astype(F32)
            d = wsum / cnt - e[POOL_HALO:POOL_HALO + sub, gs].astype(F32)
            cols.append(_dot(d.astype(BF16), pw_ref[gi]))
        yp_rows.append(jnp.concatenate(cols, axis=1))
    y_pool = (jnp.concatenate(yp_rows, axis=0) * ps_ref[...]).astype(BF16)

    gt = gt_ref[...].astype(F32)
    z = (_sigmoid(gt[:, :D_MODEL]) * _dot(y_gla, wb_ref[0])
         + _sigmoid(gt[:, D_MODEL:2 * D_MODEL]) * _dot(ya_ref[...], wb_ref[1])
         + _sigmoid(gt[:, 2 * D_MODEL:]) * _dot(y_pool, wb_ref[2]))
    y = _dot(z.astype(BF16), wo_ref[...])
    xn = x_ref[...] + g1_ref[pl.ds(b, 1), :] * y
    xo_ref[...] = xn
    ms = jnp.mean(xn * xn, axis=-1, keepdims=True)
    h2 = xn * lax.rsqrt(ms + NORM_EPS) * n2g_ref[...]
    h2 = h2 * (1.0 + sc2_ref[pl.ds(b, 1), :]) + sh2_ref[pl.ds(b, 1), :]
    h2_ref[...] = h2
    hh, hl = _split_bf16(h2)
    lg_ref[...] = _dot(hh, wrh_ref[...]) + _dot(hh, wrl_ref[...]) + _dot(hl, wrh_ref[...]) + br_ref[...]


def _merge_call(o_f, o_b, u, y_att, xin, x_row_off, B, T, mods, l, mod_row, wts, n_out_rows, prev_outs):
    tm = min(TM_MERGE, T)
    tpb = T // tm
    D = D_MODEL
    R = mods.shape[2]
    ro = x_row_off // tm
    ro16 = x_row_off // POOL_HALO
    r16 = tm // POOL_HALO
    n16 = u.shape[0] // POOL_HALO
    cgr, cpu = OFF_GR // 512, OFF_PU // 512
    n_seq_tiles = B * tpb
    n_grid = n_seq_tiles if prev_outs is not None else (n_out_rows - x_row_off) // tm
    fill_tail = n_grid > n_seq_tiles
    ic = lambda i: jnp.minimum(i, n_seq_tiles - 1)
    full = lambda shp: pl.BlockSpec(shp, lambda i: (0,) * len(shp))
    mod_spec = lambda k: pl.BlockSpec((None, None, R, D), lambda i: (l, k, 0, 0))
    in_specs = [
        pl.BlockSpec((tm, GLA_V), lambda i: (ic(i), 0)),
        pl.BlockSpec((tm, GLA_V), lambda i: (ic(i), 0)),
        pl.BlockSpec((tm, 512), lambda i: (ro + ic(i), cgr)),
        pl.BlockSpec((tm, 512), lambda i: (ro + ic(i), cpu)),
        pl.BlockSpec((POOL_HALO, 512), lambda i: (jnp.maximum(ro16 + ic(i) * r16 - 1, 0), cpu)),
        pl.BlockSpec((POOL_HALO, 512), lambda i: (jnp.minimum(ro16 + (ic(i) + 1) * r16, n16 - 1), cpu)),
        pl.BlockSpec((tm, N_BRANCH * D), lambda i: (ro + ic(i), 0)),
        pl.BlockSpec((tm, ATT_Q), lambda i: (ic(i), 0)),
        pl.BlockSpec((tm, D), lambda i: (ro + ic(i), 0)),
        mod_spec(2), mod_spec(3), mod_spec(4),
        full((1, GLA_DV)), full((4, 128, 128 + 2 * POOL_HALO)), full((4, POOL_GROUP, POOL_GROUP)), full((1, POOL_WIDTH)),
        full((N_BRANCH, BRANCH_WIDTH, D)), full((D, D)), full((1, D)), full((D, ROUTER_W)), full((D, ROUTER_W)),
        full((1, ROUTER_W)),
    ]
    args = [o_f, o_b, u, u, u, u, u, y_att, xin, mods, mods, mods, *wts]
    out_shape = [jax.ShapeDtypeStruct((n_out_rows, D), F32), jax.ShapeDtypeStruct((n_out_rows, D), F32),
                 jax.ShapeDtypeStruct((n_out_rows, ROUTER_W), F32)]
    out_specs = [pl.BlockSpec((tm, D), lambda i: (ro + i, 0)), pl.BlockSpec((tm, D), lambda i: (ro + i, 0)),
                 pl.BlockSpec((tm, ROUTER_W), lambda i: (ro + i, 0))]
    aliases = {}
    if prev_outs is not None:
        n_in = len(args)
        in_specs += [pl.BlockSpec(memory_space=pl.ANY)] * 3
        args += list(prev_outs)
        aliases = {n_in: 0, n_in + 1: 1, n_in + 2: 2}
    kern = functools.partial(_merge_kernel, n_seq_tiles=n_seq_tiles, fill_tail=fill_tail,
                             T=T, tm=tm, tiles_per_batch=tpb, row_base=mod_row)
    return pl.pallas_call(
        kern,
        grid=(n_grid,),
        in_specs=in_specs,
        out_specs=out_specs,
        out_shape=out_shape,
        input_output_aliases=aliases,
        compiler_params=_cparams(("arbitrary",)),
        name="merge",
    )(*args)


def _moe_kernel(be_ref, nv_ref, nu_ref, idx_hbm, x_hbm, wg_ref, wu_ref, wd_ref, y_hbm,
                idx_s, isem, xbuf, gsem, ybuf, ssem, *, bm):
    i = pl.program_id(0)
    nu = nu_ref[0]

    def idx_copy(j):
        s = j % 3
        return pltpu.make_async_copy(idx_hbm.at[j], idx_s.at[s], isem.at[s])

    def row_in(j, r, tok):
        s = j % 2
        return pltpu.make_async_copy(x_hbm.at[pl.ds(tok, 1), :], xbuf.at[s, pl.ds(r, 1), :], gsem.at[s])

    def row_out(j, r, row):
        s = j % 2
        return pltpu.make_async_copy(ybuf.at[s, pl.ds(r, 1), :], y_hbm.at[pl.ds(row, 1), :], ssem.at[s])

    def start_gather(j):
        def body(r, c):
            row_in(j, r, idx_s[j % 3, 0, r]).start()
            return c
        lax.fori_loop(0, bm, body, 0)

    def wait_gather(j):
        def body(r, c):
            row_in(j, r, 0).wait()
            return c
        lax.fori_loop(0, bm, body, 0)

    def start_scatter(j):
        def body(r, c):
            row_out(j, r, idx_s[j % 3, 0, bm + r]).start()
            return c
        lax.fori_loop(0, nv_ref[j], body, 0)

    def wait_scatter(j):
        def body(r, c):
            row_out(j, r, 0).wait()
            return c
        lax.fori_loop(0, nv_ref[j], body, 0)

    @pl.when(i == 0)
    def _():
        idx_copy(0).start()
        idx_copy(0).wait()
        start_gather(0)

        @pl.when(nu > 1)
        def _():
            idx_copy(1).start()

    @pl.when(i + 1 < nu)
    def _():
        idx_copy(i + 1).wait()
        start_gather(i + 1)

    @pl.when(i + 2 < nu)
    def _():
        idx_copy(i + 2).start()

    @pl.when(i < nu)
    def _():
        wait_gather(i)
        s = i % 2
        xb = xbuf[s].astype(BF16)
        g = _dot(xb, wg_ref[...])
        up = _dot(xb, wu_ref[...])
        hmid = (_silu(g) * up).astype(BF16)
        ybuf[s] = _dot(hmid, wd_ref[...])
        start_scatter(i)

        @pl.when(i > 0)
        def _():
            wait_scatter(i - 1)

        @pl.when(i == nu - 1)
        def _():
            wait_scatter(i)


def _moe_call(blk_expert, nvalid, nused, idx, h2, wg, wu, wd, n_y_rows):
    nblk = idx.shape[0]
    bm = BM_MOE
    D = D_MODEL
    grid_spec = pltpu.PrefetchScalarGridSpec(
        num_scalar_prefetch=3,
        grid=(nblk,),
        in_specs=[
            pl.BlockSpec(memory_space=pl.ANY),
            pl.BlockSpec(memory_space=pl.ANY),
            pl.BlockSpec((None, D, D_EXPERT), lambda i, be, nv, nu: (be[i], 0, 0)),
            pl.BlockSpec((None, D, D_EXPERT), lambda i, be, nv, nu: (be[i], 0, 0)),
            pl.BlockSpec((None, D_EXPERT, D), lambda i, be, nv, nu: (be[i], 0, 0)),
        ],
        out_specs=pl.BlockSpec(memory_space=pl.ANY),
        scratch_shapes=[
            pltpu.SMEM((3, 1, 2 * bm), jnp.int32),
            pltpu.SemaphoreType.DMA((3,)),
            pltpu.VMEM((2, bm, D), F32),
            pltpu.SemaphoreType.DMA((2,)),
            pltpu.VMEM((2, bm, D), F32),
            pltpu.SemaphoreType.DMA((2,)),
        ],
    )
    return pl.pallas_call(
        functools.partial(_moe_kernel, bm=bm),
        grid_spec=grid_spec,
        out_shape=jax.ShapeDtypeStruct((n_y_rows, D), F32),
        compiler_params=pltpu.CompilerParams(dimension_semantics=("arbitrary",), vmem_limit_bytes=VMEM_LIMIT_BYTES,
                                             has_side_effects=True),
        name="moe_experts",
    )(blk_expert, nvalid, nused, idx, h2, wg, wu, wd)


def _combine_kernel(x_ref, y0_ref, y1_ref, w0_ref, w1_ref, g2_ref, fg_ref, o_ref, *,
                    n_lat_tiles, tiles_per_batch, ctx_row, final):
    i = pl.program_id(0)
    b = jnp.where(i < n_lat_tiles, i // tiles_per_batch, ctx_row)
    m = w0_ref[...] * y0_ref[...] + w1_ref[...] * y1_ref[...]
    xn = x_ref[...] + g2_ref[pl.ds(b, 1), :] * m
    if final:
        ms = jnp.mean(xn * xn, axis=-1, keepdims=True)
        xn = xn * lax.rsqrt(ms + NORM_EPS) * fg_ref[...]
    o_ref[...] = xn


def _combine_call(x_mid, y2, w0, w1, mods, l, final_g, n_rows, n_lat_rows, rows_per_batch, ctx_row, final):
    tm = TM_COMB
    D = D_MODEL
    R = mods.shape[2]
    half = n_rows // tm
    kern = functools.partial(_combine_kernel, n_lat_tiles=n_lat_rows // tm, tiles_per_batch=rows_per_batch // tm,
                             ctx_row=ctx_row, final=final)
    return pl.pallas_call(
        kern,
        grid=(n_rows // tm,),
        in_specs=[
            pl.BlockSpec((tm, D), lambda i: (i, 0)),
            pl.BlockSpec((tm, D), lambda i: (i, 0)),
            pl.BlockSpec((tm, D), lambda i: (half + i, 0)),
            pl.BlockSpec((tm, 1), lambda i: (i, 0)),
            pl.BlockSpec((tm, 1), lambda i: (i, 0)),
            pl.BlockSpec((None, None, R, D), lambda i: (l, 5, 0, 0)),
            pl.BlockSpec((1, D), lambda i: (0, 0)),
        ],
        out_specs=pl.BlockSpec((tm, D), lambda i: (i, 0)),
        out_shape=jax.ShapeDtypeStruct((n_rows, D), F32),
        compiler_params=_cparams(("arbitrary",)),
        name="combine",
    )(x_mid, y2, y2, w0, w1, mods, final_g.reshape(1, D))


def _permute_w_in(w_in):
    gq, gk, gv, gr, glf, glb, aq, ak, av, pu, gt = jnp.split(w_in, np.cumsum(IN_SIZES)[:-1].tolist(), axis=-1)
    pad = jnp.zeros(w_in.shape[:-1] + (U_WIDTH - sum(IN_SIZES),), w_in.dtype)
    return jnp.concatenate([gt, gq, gk, gv, gr, aq, pu, ak, av, glf, glb, pad], axis=-1).astype(BF16)


def _rope_tables(T):
    rows = T // GRID_W
    row = np.repeat(np.arange(rows), GRID_W).astype(np.float32)
    col = np.tile(np.arange(GRID_W), rows).astype(np.float32)
    inv = jnp.asarray(ROPE_THETA, F32) ** (-jnp.arange(0, ROPE_AXIS_DIM, 2, dtype=F32) / ROPE_AXIS_DIM)
    ang_r = jnp.asarray(row)[:, None] * inv
    ang_c = jnp.asarray(col)[:, None] * inv
    zero = jnp.zeros_like(ang_r)
    cos = jnp.concatenate([jnp.cos(ang_r)] * 2 + [jnp.cos(ang_c)] * 2, axis=1)
    s1 = jnp.concatenate([-jnp.sin(ang_r), zero, -jnp.sin(ang_c), zero], axis=1)
    s2 = jnp.concatenate([zero, jnp.sin(ang_r), zero, jnp.sin(ang_c)], axis=1)
    return tuple(jnp.concatenate([t, t], axis=1) for t in (cos, s1, s2))


def _pool_bands():
    i = np.arange(128)[:, None]
    j = np.arange(128 + 2 * POOL_HALO)[None, :]
    bands = [((j >= i + POOL_HALO - w // 2) & (j < i + POOL_HALO + w // 2)).astype(np.float32) for w in POOL_WINDOWS]
    return jnp.asarray(np.stack(bands), BF16)


def _route(logits, n_rows, bm):
    N = n_rows
    lg = logits[:, :N_GROUPS]
    le = logits[:, N_GROUPS:N_GROUPS + N_EXPERTS].reshape(N, N_GROUPS, EXP_PER_GROUP)
    p_group = jax.nn.softmax(lg, axis=-1)
    p_top_group, grp = lax.top_k(p_group, 1)
    grp = grp[:, 0]
    p_in = jax.nn.softmax(jnp.take_along_axis(le, grp[:, None, None], axis=1)[:, 0], axis=-1)
    p_top, idx = lax.top_k(p_in, TOP_K)
    wts = p_top_group * p_top / jnp.sum(p_top, axis=-1, keepdims=True)
    eid = (grp[:, None] * EXP_PER_GROUP + idx).reshape(-1).astype(jnp.int32)
    M = N * TOP_K
    onehot = (eid[:, None] == jnp.arange(N_EXPERTS, dtype=jnp.int32)[None, :]).astype(jnp.int32)
    csum = jnp.cumsum(onehot, axis=0)
    pos = jnp.take_along_axis(csum, eid[:, None], axis=1)[:, 0] - 1
    counts = csum[-1]
    padded = (counts + bm - 1) // bm * bm
    pad_end = jnp.cumsum(padded)
    pad_start = pad_end - padded
    dest = pad_start[eid] + pos
    nblk = -(-(M + N_EXPERTS * (bm - 1)) // bm)
    P = nblk * bm
    aid = jnp.arange(M, dtype=jnp.int32)
    tok = aid // TOP_K
    kk = aid % TOP_K
    gat = jnp.zeros((P,), jnp.int32).at[dest].set(tok)
    sca = jnp.zeros((P,), jnp.int32).at[dest].set(kk * N + tok)
    idx_arr = jnp.concatenate([gat.reshape(nblk, 1, bm), sca.reshape(nblk, 1, bm)], axis=2)
    blk_start = jnp.arange(nblk, dtype=jnp.int32) * bm
    blk_expert = jnp.minimum(jnp.searchsorted(pad_end, blk_start, side='right'), N_EXPERTS - 1).astype(jnp.int32)
    nvalid = jnp.clip(pad_start[blk_expert] + counts[blk_expert] - blk_start, 0, bm).astype(jnp.int32)
    nused = (pad_end[-1] // bm).astype(jnp.int32).reshape(1)
    return wts[:, 0:1], wts[:, 1:2], idx_arr, blk_expert, nvalid, nused


def kernel(x, c, ctx, c_ctx, w_mod, b_mod, norm1_g, norm2_g, w_in, gla_a_up_f, gla_a_bias_f, gla_a_up_b, gla_a_bias_b, gla_norm_g, att_qn_g, att_kn_g, pool_w, pool_scale, w_branch, w_out, moe_w_group, moe_b_group, moe_w_expert, moe_b_expert, moe_w_gate, moe_w_up, moe_w_down, final_g):
    B, T, D = x.shape
    C = ctx.shape[1]
    L = w_mod.shape[0]
    n_lat, n_ctx = B * T, B * C
    MOD_ROWS = 16
    assert D == D_MODEL and B < MOD_ROWS and T % TQ_ATT == 0 and C % GLA_CHUNK == 0

    s_in = jnp.concatenate([c, c_ctx[None], jnp.zeros((MOD_ROWS - B - 1, D), F32)], axis=0)
    mods = _mod_call(s_in, w_mod, b_mod)
    w_perm = _permute_w_in(w_in)
    tabs = _rope_tables(T)
    bands = _pool_bands()
    zero_state = jnp.zeros((B, GLA_V, GLA_QK), F32)

    xall = jnp.concatenate([x.reshape(n_lat, D), ctx.reshape(n_ctx, D)], axis=0)
    out = None
    for l in range(L):
        want_ctx = l < L - 1
        n_rows = xall.shape[0]
        u = _inproj_call(xall, mods, l, norm1_g[l], w_perm[l], n_lat, T, B)

        up = jnp.zeros((128, 2 * GLA_QK), F32)
        up = up.at[:GLA_RANK, :GLA_QK].set(gla_a_up_f[l]).at[GLA_RANK:2 * GLA_RANK, GLA_QK:].set(gla_a_up_b[l]).astype(BF16)
        bias = jnp.concatenate([gla_a_bias_f[l], gla_a_bias_b[l]]).reshape(1, 2 * GLA_QK)
        ofc, obc, sfc, sbc = _gla_call(u, n_lat, B, C, zero_state, zero_state, up, bias)
        of, ob, _, _ = _gla_call(u, 0, B, T, sfc, sbc, up, bias)

        qg = jnp.tile(att_qn_g[l], 2).reshape(1, LANES)
        kg = jnp.tile(att_kn_g[l], 2).reshape(1, LANES)
        ya = _att_call(u, 0, B, T, T, C, n_lat, tabs, qg, kg)

        wr = jnp.zeros((D, ROUTER_W), F32).at[:, :N_GROUPS].set(moe_w_group[l]).at[:, N_GROUPS:N_GROUPS + N_EXPERTS].set(moe_w_expert[l])
        wrh, wrl = _split_bf16(wr)
        br = jnp.zeros((1, ROUTER_W), F32).at[0, :N_GROUPS].set(moe_b_group[l]).at[0, N_GROUPS:N_GROUPS + N_EXPERTS].set(moe_b_expert[l])
        wts = (gla_norm_g[l].reshape(1, GLA_DV), bands, pool_w[l].astype(BF16), pool_scale[l].reshape(1, POOL_WIDTH),
               w_branch[l].astype(BF16), w_out[l].astype(BF16), norm2_g[l].reshape(1, D), wrh, wrl, br)
        n_tok = n_rows if want_ctx else n_lat
        outs = _merge_call(of, ob, u, ya, xall, 0, B, T, mods, l, None, wts, n_tok, None)
        if want_ctx:
            yac = _att_call(u, n_lat, B, C, 0, C, n_lat, None, qg, kg)
            outs = _merge_call(ofc, obc, u, yac, xall, n_lat, B, C, mods, l, B, wts, n_tok, outs)
        x_mid, h2, logits = outs

        w0, w1, idx_arr, blk_expert, nvalid, nused = _route(logits, n_tok, BM_MOE)
        y2 = _moe_call(blk_expert, nvalid, nused, idx_arr, h2, moe_w_gate[l].astype(BF16), moe_w_up[l].astype(BF16),
                       moe_w_down[l].astype(BF16), TOP_K * n_tok)
        xall = _combine_call(x_mid, y2, w0, w1, mods, l, final_g, n_tok, n_lat, T, B, final=not want_ctx)
    return xall[:n_lat].reshape(B, T, D)
```

```python
import functools

import numpy as np
import jax
import jax.numpy as jnp
from jax import lax
from jax.experimental import pallas as pl
from jax.experimental.pallas import tpu as pltpu

F32 = jnp.float32
BF16 = jnp.bfloat16

VMEM_LIMIT_BYTES = 56 * 1024 * 1024
LANES = 128

D_MODEL = 1024
GRID_W = 64
NORM_EPS = 1e-6
GLA_HEADS, GLA_DK, GLA_DV, GLA_RANK, GLA_TAU, GLA_CHUNK = 4, 64, 128, 16, 16.0, 64
GLA_QK, GLA_V = GLA_HEADS * GLA_DK, GLA_HEADS * GLA_DV
ATT_HEADS, ATT_KV_HEADS, ATT_DH = 8, 2, 64
ROPE_THETA, ROPE_AXIS_DIM = 10000.0, 32
ATT_Q, ATT_KV = ATT_HEADS * ATT_DH, ATT_KV_HEADS * ATT_DH
POOL_WINDOWS, POOL_GROUP = (2, 4, 8, 16), 128
POOL_WIDTH = POOL_GROUP * len(POOL_WINDOWS)
POOL_HALO = 16
N_BRANCH, BRANCH_WIDTH = 3, 512
N_GROUPS, EXP_PER_GROUP, TOP_K, D_EXPERT = 4, 8, 2, 512
N_EXPERTS = N_GROUPS * EXP_PER_GROUP
IN_SIZES = (GLA_QK, GLA_QK, GLA_V, GLA_V, GLA_RANK, GLA_RANK, ATT_Q, ATT_KV, ATT_KV, POOL_WIDTH, N_BRANCH * D_MODEL)

OFF_GT, OFF_GQK, OFF_GV, OFF_GR, OFF_AQ, OFF_PU, OFF_AKV, OFF_GL = 0, 3072, 3584, 4096, 4608, 5120, 5632, 5888
U_WIDTH = 6144
U_CHUNK = 512

TM_IN = 512
TB_GLA = 256
TQ_ATT = 512
TK_ATT = 512
TM_MERGE = 512
BM_MOE = 256
TM_ROUTE = 512
TM_DISP = 256
TM_COMB = 256
ROUTER_W = 128
TOK_ROWS = D_MODEL // LANES
DMA_UNROLL = 8


def _cparams(sem):
    return pltpu.CompilerParams(dimension_semantics=sem, vmem_limit_bytes=VMEM_LIMIT_BYTES)


def _split_bf16(a):
    hi = a.astype(BF16)
    lo = (a - hi.astype(F32)).astype(BF16)
    return hi, lo


def _dot(a, b):
    return jnp.dot(a, b, preferred_element_type=F32)


def _dot_nt(a, b):
    return lax.dot_general(a, b, (((1,), (1,)), ((), ())), preferred_element_type=F32)


def _dot_tn(a, b):
    return lax.dot_general(a, b, (((0,), (0,)), ((), ())), preferred_element_type=F32)


def _dot3(a, b):
    ah, al = _split_bf16(a)
    bh, bl = _split_bf16(b)
    return _dot(ah, bh) + _dot(ah, bl) + _dot(al, bh)


def _load_token_rows(ref, n_tok, row0=0):
    return jnp.concatenate([ref[pl.ds(row0 + s, n_tok, stride=TOK_ROWS), :] for s in range(TOK_ROWS)], axis=1)


def _store_token_rows(ref, val):
    n_tok = val.shape[0]
    for s in range(TOK_ROWS):
        ref[pl.ds(s, n_tok, stride=TOK_ROWS), :] = val[:, s * LANES:(s + 1) * LANES]


def _sigmoid(x):
    return 1.0 / (1.0 + jnp.exp(-x))


def _silu(x):
    return x * _sigmoid(x)


def _mod_kernel(s_ref, w_ref, b_ref, o_ref):
    s = _silu(s_ref[...])
    o_ref[...] = _dot3(s, w_ref[...]) + b_ref[...]


def _mod_call(s_in, w_mod, b_mod):
    L, D, _ = w_mod.shape
    R = s_in.shape[0]
    return pl.pallas_call(
        _mod_kernel,
        grid=(L, 6),
        in_specs=[
            pl.BlockSpec((R, D), lambda l, j: (0, 0)),
            pl.BlockSpec((None, D, D), lambda l, j: (l, 0, j)),
            pl.BlockSpec((None, None, 1, D), lambda l, j: (l, j, 0, 0)),
        ],
        out_specs=pl.BlockSpec((None, None, R, D), lambda l, j: (l, j, 0, 0)),
        out_shape=jax.ShapeDtypeStruct((L, 6, R, D), F32),
        compiler_params=_cparams(("arbitrary", "arbitrary")),
        name="mod_table",
    )(s_in, w_mod, b_mod.reshape(L, 6, 1, D))


def _inproj_kernel(x_ref, sh_ref, sc_ref, g_ref, w_ref, o_ref, *, n_lat_tiles, tiles_per_batch, ctx_row):
    i = pl.program_id(0)
    b = jnp.where(i < n_lat_tiles, i // tiles_per_batch, ctx_row)
    x = x_ref[...]
    ms = jnp.mean(x * x, axis=-1, keepdims=True)
    h = x * lax.rsqrt(ms + NORM_EPS) * g_ref[...]
    h = h * (1.0 + sc_ref[pl.ds(b, 1), :]) + sh_ref[pl.ds(b, 1), :]
    hb = h.astype(BF16)
    for c in range(U_WIDTH // U_CHUNK):
        cs = slice(c * U_CHUNK, (c + 1) * U_CHUNK)
        o_ref[:, cs] = _dot(hb, w_ref[:, cs]).astype(BF16)


def _inproj_call(x, mods, l, norm_g, w_perm, n_lat_rows, rows_per_batch, ctx_row):
    N, D = x.shape
    R = mods.shape[2]
    tm = TM_IN
    kern = functools.partial(_inproj_kernel, n_lat_tiles=n_lat_rows // tm,
                             tiles_per_batch=rows_per_batch // tm, ctx_row=ctx_row)
    return pl.pallas_call(
        kern,
        grid=(N // tm,),
        in_specs=[
            pl.BlockSpec((tm, D), lambda i: (i, 0)),
            pl.BlockSpec((None, None, R, D), lambda i: (l, 0, 0, 0)),
            pl.BlockSpec((None, None, R, D), lambda i: (l, 1, 0, 0)),
            pl.BlockSpec((1, D), lambda i: (0, 0)),
            pl.BlockSpec((D, U_WIDTH), lambda i: (0, 0)),
        ],
        out_specs=pl.BlockSpec((tm, U_WIDTH), lambda i: (i, 0)),
        out_shape=jax.ShapeDtypeStruct((N, U_WIDTH), BF16),
        compiler_params=_cparams(("arbitrary",)),
        name="inproj",
    )(x, mods, mods, norm_g.reshape(1, D), w_perm)


def _gla_chunk(qk, v, gl, up, bias, st_ref, forward):
    Lc = GLA_CHUNK
    q = qk[:, :GLA_QK].astype(F32) * (GLA_DK ** -0.5)
    k = qk[:, GLA_QK:].astype(F32)
    z = _dot(gl, up) + bias
    la = (jnp.minimum(z, 0.0) - jnp.log(1.0 + jnp.exp(-jnp.abs(z)))) * (1.0 / GLA_TAU)
    r = lax.broadcasted_iota(jnp.int32, (Lc, Lc), 0)
    c = lax.broadcasted_iota(jnp.int32, (Lc, Lc), 1)
    tri = jnp.where((r >= c) if forward else (r <= c), 1.0, 0.0).astype(BF16)
    la_hi, la_lo = _split_bf16(la)
    b = _dot(tri, la_hi) + _dot(tri, la_lo)
    i_last, i_ref = (Lc - 1, Lc // 2) if forward else (0, Lc - 1 - Lc // 2)
    b_last = b[i_last:i_last + 1, :]
    b_ref = b[i_ref:i_ref + 1, :]
    qd = (q * jnp.exp(b - b_ref)).astype(BF16)
    kd = (k * jnp.exp(b_ref - b)).astype(BF16)
    qe = (q * jnp.exp(b)).astype(BF16)
    kl = (k * jnp.exp(b_last - b)).astype(BF16)
    a = jnp.exp(b_last)
    lane_head = lax.broadcasted_iota(jnp.int32, (Lc, GLA_QK), 1) // GLA_DK
    kd_blk = jnp.concatenate([jnp.where(lane_head == h, kd, jnp.zeros_like(kd)) for h in range(GLA_HEADS)], axis=0)
    sc = _dot_nt(qd, kd_blk)
    t_i = lax.broadcasted_iota(jnp.int32, (Lc, GLA_HEADS * Lc), 0)
    s_i = lax.broadcasted_iota(jnp.int32, (Lc, GLA_HEADS * Lc), 1) % Lc
    sc = jnp.where((s_i <= t_i) if forward else (s_i >= t_i), sc, 0.0).astype(BF16)
    vlane_head = lax.broadcasted_iota(jnp.int32, (Lc, GLA_V), 1) // GLA_DV
    v_blk = jnp.concatenate([jnp.where(vlane_head == h, v, jnp.zeros_like(v)) for h in range(GLA_HEADS)], axis=0)
    st = st_ref[...]
    o = _dot(sc, v_blk) + _dot_nt(qe, st.astype(BF16))
    ds = _dot_tn(v, kl)
    row_head = lax.broadcasted_iota(jnp.int32, (GLA_V, GLA_QK), 0) // GLA_DV
    col_head = lax.broadcasted_iota(jnp.int32, (GLA_V, GLA_QK), 1) // GLA_DK
    st_ref[...] = a * st + jnp.where(row_head == col_head, ds, 0.0)
    return o


def _gla_kernel(qkf_ref, vf_ref, glf_ref, qkb_ref, vb_ref, glb_ref, up_ref, bias_ref, s0f_ref, s0b_ref,
                of_ref, ob_ref, sf_ref, sb_ref, stf, stb, *, nsub):
    n = pl.program_id(1)

    @pl.when(n == 0)
    def _():
        stf[...] = s0f_ref[...]
        stb[...] = s0b_ref[...]

    def body(j, carry):
        rf = pl.ds(pl.multiple_of(j * GLA_CHUNK, GLA_CHUNK), GLA_CHUNK)
        of_ref[rf, :] = _gla_chunk(qkf_ref[rf, :], vf_ref[rf, :], glf_ref[rf, :], up_ref[:, :GLA_QK],
                                   bias_ref[:, :GLA_QK], stf, True)
        rb = pl.ds(pl.multiple_of((nsub - 1 - j) * GLA_CHUNK, GLA_CHUNK), GLA_CHUNK)
        ob_ref[rb, :] = _gla_chunk(qkb_ref[rb, :], vb_ref[rb, :], glb_ref[rb, :], up_ref[:, GLA_QK:],
                                   bias_ref[:, GLA_QK:], stb, False)
        return carry

    lax.fori_loop(0, nsub, body, 0)

    @pl.when(n == pl.num_programs(1) - 1)
    def _():
        sf_ref[...] = stf[...]
        sb_ref[...] = stb[...]


def _gla_call(u, row_off, B, T, s0f, s0b, up, bias):
    tb = min(TB_GLA, T)
    nb = T // tb
    off = row_off // tb
    fwd = lambda b, n: off + b * nb + n
    bwd = lambda b, n: off + b * nb + (nb - 1 - n)
    cqk, cv, cgl = OFF_GQK // 512, OFF_GV // 512, OFF_GL // 128
    st_spec = pl.BlockSpec((None, GLA_V, GLA_QK), lambda b, n: (b, 0, 0))
    st_shape = jax.ShapeDtypeStruct((B, GLA_V, GLA_QK), F32)
    o_shape = jax.ShapeDtypeStruct((B * T, GLA_V), F32)
    return pl.pallas_call(
        functools.partial(_gla_kernel, nsub=tb // GLA_CHUNK),
        grid=(B, nb),
        in_specs=[
            pl.BlockSpec((tb, 512), lambda b, n: (fwd(b, n), cqk)),
            pl.BlockSpec((tb, 512), lambda b, n: (fwd(b, n), cv)),
            pl.BlockSpec((tb, 128), lambda b, n: (fwd(b, n), cgl)),
            pl.BlockSpec((tb, 512), lambda b, n: (bwd(b, n), cqk)),
            pl.BlockSpec((tb, 512), lambda b, n: (bwd(b, n), cv)),
            pl.BlockSpec((tb, 128), lambda b, n: (bwd(b, n), cgl)),
            pl.BlockSpec((128, 2 * GLA_QK), lambda b, n: (0, 0)),
            pl.BlockSpec((1, 2 * GLA_QK), lambda b, n: (0, 0)),
            st_spec, st_spec,
        ],
        out_specs=[
            pl.BlockSpec((tb, GLA_V), lambda b, n: (b * nb + n, 0)),
            pl.BlockSpec((tb, GLA_V), lambda b, n: (b * nb + (nb - 1 - n), 0)),
            st_spec, st_spec,
        ],
        out_shape=[o_shape, o_shape, st_shape, st_shape],
        scratch_shapes=[pltpu.VMEM((GLA_V, GLA_QK), F32), pltpu.VMEM((GLA_V, GLA_QK), F32)],
        compiler_params=_cparams(("arbitrary", "arbitrary")),
        name="gla_scan",
    )(u, u, u, u, u, u, up, bias, s0f, s0b)


def _group_ms64(x):
    i = lax.broadcasted_iota(jnp.int32, (LANES, LANES), 0) // ATT_DH
    j = lax.broadcasted_iota(jnp.int32, (LANES, LANES), 1) // ATT_DH
    bd = jnp.where(i == j, 1.0, 0.0).astype(BF16)
    hi, lo = _split_bf16(x * x)
    return (_dot(hi, bd) + _dot(lo, bd)) * (1.0 / ATT_DH)


def _rope128(x, cos, s1, s2):
    return x * cos + pltpu.roll(x, LANES - ROPE_AXIS_DIM // 2, axis=1) * s1 + pltpu.roll(x, ROPE_AXIS_DIM // 2, axis=1) * s2


def _att_kernel(*refs, T, C, tk, use_rope):
    if T:
        (q_ref, kvl_ref, kvc_ref, cq_ref, s1q_ref, s2q_ref, ck_ref, s1k_ref, s2k_ref, qg_ref, kg_ref,
         o_ref, kx, vx) = refs
    else:
        q_ref, kvc_ref, qg_ref, kg_ref, o_ref, kx, vx = refs
    i = pl.program_id(1)
    lo_half = lax.broadcasted_iota(jnp.int32, (1, LANES), 1) < ATT_DH

    def put_kv(rows, kv, rope_tabs):
        k = kv[:, :LANES].astype(F32)
        v = kv[:, LANES:].astype(F32)
        k = k * lax.rsqrt(_group_ms64(k) + NORM_EPS) * kg_ref[...]
        if rope_tabs is not None:
            k = _rope128(k, *rope_tabs)
        k_sw = pltpu.roll(k, ATT_DH, axis=1)
        v_sw = pltpu.roll(v, ATT_DH, axis=1)
        for (a, a_sw, dst) in ((k, k_sw, kx), (v, v_sw, vx)):
            dst[0, rows, :] = jnp.where(lo_half, a, a_sw).astype(BF16)
            dst[1, rows, :] = jnp.where(lo_half, a_sw, a).astype(BF16)

    @pl.when(i == 0)
    def _():
        if T:
            def body(j, carry):
                rows = pl.ds(pl.multiple_of(j * tk, tk), tk)
                put_kv(rows, kvl_ref[rows, :], (ck_ref[rows, :], s1k_ref[rows, :], s2k_ref[rows, :]))
                return carry
            lax.fori_loop(0, T // tk, body, 0)
        put_kv(pl.ds(T, C), kvc_ref[...], None)

    n_slab = ATT_Q // LANES
    q_slabs = []
    for s in range(n_slab):
        q = q_ref[:, s * LANES:(s + 1) * LANES].astype(F32)
        q = q * lax.rsqrt(_group_ms64(q) + NORM_EPS) * qg_ref[...]
        if use_rope:
            q = _rope128(q, cq_ref[...], s1q_ref[...], s2q_ref[...])
        q_slabs.append((q * (ATT_DH ** -0.5)).astype(BF16))

    tq = q_ref.shape[0]
    slabs_per_kv = n_slab // ATT_KV_HEADS
    for s in range(n_slab):
        g = s // slabs_per_kv
        qs = (jnp.where(lo_half, q_slabs[s], jnp.zeros_like(q_slabs[s])),
              jnp.where(lo_half, jnp.zeros_like(q_slabs[s]), q_slabs[s]))

        def step(rows, carry):
            kc = kx[g, rows, :]
            vc = vx[g, rows, :]
            new = []
            for hq, (m, l, acc) in zip(qs, carry):
                sc = _dot_nt(hq, kc)
                m_new = jnp.maximum(m, jnp.max(sc, axis=-1, keepdims=True))
                alpha = jnp.exp(m - m_new)
                p = jnp.exp(sc - m_new)
                l = alpha * l + jnp.sum(p, axis=-1, keepdims=True)
                acc = alpha * acc + _dot(p.astype(BF16), vc)
                new.append((m_new, l, acc))
            return tuple(new)

        init = (jnp.full((tq, 1), -jnp.inf, F32), jnp.zeros((tq, 1), F32), jnp.zeros((tq, LANES), F32))
        carry = (init, init)
        if T:
            carry = lax.fori_loop(
                0, T // tk, lambda j, cr: step(pl.ds(pl.multiple_of(j * tk, tk), tk), cr), carry)
        (_, l0, a0), (_, l1, a1) = step(pl.ds(T, C), carry)
        o_ref[:, s * LANES:(s + 1) * LANES] = jnp.where(lo_half, a0 / l0, a1 / l1).astype(BF16)


def _att_call(u, q_row_off, B, Tq, T, C, ctx_row_off, tabs, qg, kg):
    tq = min(TQ_ATT, Tq)
    nq = Tq // tq
    tk = TK_ATT
    S = T + C
    cq, ckv = OFF_AQ // 512, OFF_AKV // 256
    qoff = q_row_off // tq
    coff = ctx_row_off // C
    g_spec = pl.BlockSpec((1, LANES), lambda b, i: (0, 0))
    q_spec = pl.BlockSpec((tq, ATT_Q), lambda b, i: (qoff + b * nq + i, cq))
    kvc_spec = pl.BlockSpec((C, 2 * ATT_KV), lambda b, i: (coff + b, ckv))
    if T:
        tq_tab = pl.BlockSpec((tq, LANES), lambda b, i: (i, 0))
        tk_tab = pl.BlockSpec((T, LANES), lambda b, i: (0, 0))
        in_specs = [q_spec, pl.BlockSpec((T, 2 * ATT_KV), lambda b, i: (b, ckv)), kvc_spec,
                    tq_tab, tq_tab, tq_tab, tk_tab, tk_tab, tk_tab, g_spec, g_spec]
        args = (u, u, u, tabs[0], tabs[1], tabs[2], tabs[0], tabs[1], tabs[2], qg, kg)
    else:
        in_specs = [q_spec, kvc_spec, g_spec, g_spec]
        args = (u, u, qg, kg)
    return pl.pallas_call(
        functools.partial(_att_kernel, T=T, C=C, tk=tk, use_rope=bool(T)),
        grid=(B, nq),
        in_specs=in_specs,
        out_specs=pl.BlockSpec((tq, ATT_Q), lambda b, i: (b * nq + i, 0)),
        out_shape=jax.ShapeDtypeStruct((B * Tq, ATT_Q), BF16),
        scratch_shapes=[pltpu.VMEM((ATT_KV_HEADS, S, LANES), BF16), pltpu.VMEM((ATT_KV_HEADS, S, LANES), BF16)],
        compiler_params=_cparams(("arbitrary", "arbitrary")),
        name="gqa_lat" if T else "gqa_ctx",
    )(*args)


def _merge_kernel(*refs, n_seq_tiles, fill_tail, **kw):
    xo_ref, h2_ref, lg_ref = refs[-3:]
    if not fill_tail:
        _merge_body(*refs, **kw)
        return
    i = pl.program_id(0)

    @pl.when(i < n_seq_tiles)
    def _():
        _merge_body(*refs, **kw)

    @pl.when(i >= n_seq_tiles)
    def _():
        for r in (xo_ref, h2_ref, lg_ref):
            r[...] = jnp.zeros_like(r)


def _merge_body(*refs, T, tm, tiles_per_batch, row_base):
    (of_ref, ob_ref, gr_ref, pu_ref, pp_ref, pn_ref, gt_ref, ya_ref, x_ref, g1_ref, sh2_ref, sc2_ref,
     gng_ref, band_ref, pw_ref, ps_ref, wb_ref, wo_ref, n2g_ref, wrh_ref, wrl_ref, br_ref) = refs[:22]
    xo_ref, h2_ref, lg_ref = refs[-3:]
    i = pl.program_id(0)
    it = i % tiles_per_batch
    b = i // tiles_per_batch if row_base is None else row_base

    o = of_ref[...] + ob_ref[...]
    gr = gr_ref[...].astype(F32)
    parts = []
    for h in range(GLA_HEADS):
        hs = slice(h * GLA_DV, (h + 1) * GLA_DV)
        oh = o[:, hs]
        ms = jnp.mean(oh * oh, axis=-1, keepdims=True)
        parts.append(oh * lax.rsqrt(ms + NORM_EPS) * gng_ref[...])
    y_gla = (jnp.concatenate(parts, axis=1) * _silu(gr)).astype(BF16)

    prev = jnp.where(it > 0, pp_ref[...], jnp.zeros_like(pp_ref[...]))
    nxt = jnp.where(it < tiles_per_batch - 1, pn_ref[...], jnp.zeros_like(pn_ref[...]))
    ext = jnp.concatenate([prev, pu_ref[...], nxt], axis=0)
    sub = 128
    yp_rows = []
    for r in range(tm // sub):
        e = ext[r * sub:r * sub + sub + 2 * POOL_HALO, :]
        t = it * tm + r * sub + lax.broadcasted_iota(jnp.int32, (sub, 1), 0)
        cols = []
        for gi, win in enumerate(POOL_WINDOWS):
            gs = slice(gi * POOL_GROUP, (gi + 1) * POOL_GROUP)
            wsum = _dot(band_ref[gi], e[:, gs])
            cnt = (jnp.minimum(t + win // 2, T) - jnp.maximum(t - win // 2, 0)).astype(F32)
            d = wsum / cnt - e[POOL_HALO:POOL_HALO + sub, gs].astype(F32)
            cols.append(_dot(d.astype(BF16), pw_ref[gi]))
        yp_rows.append(jnp.concatenate(cols, axis=1))
    y_pool = (jnp.concatenate(yp_rows, axis=0) * ps_ref[...]).astype(BF16)

    gt = gt_ref[...].astype(F32)
    z = (_sigmoid(gt[:, :D_MODEL]) * _dot(y_gla, wb_ref[0])
         + _sigmoid(gt[:, D_MODEL:2 * D_MODEL]) * _dot(ya_ref[...], wb_ref[1])
         + _sigmoid(gt[:, 2 * D_MODEL:]) * _dot(y_pool, wb_ref[2]))
    y = _dot(z.astype(BF16), wo_ref[...])
    xn = x_ref[...] + g1_ref[pl.ds(b, 1), :] * y
    xo_ref[...] = xn
    ms = jnp.mean(xn * xn, axis=-1, keepdims=True)
    h2 = xn * lax.rsqrt(ms + NORM_EPS) * n2g_ref[...]
    h2 = h2 * (1.0 + sc2_ref[pl.ds(b, 1), :]) + sh2_ref[pl.ds(b, 1), :]
    _store_token_rows(h2_ref, h2)
    hh, hl = _split_bf16(h2)
    lg_ref[...] = _dot(hh, wrh_ref[...]) + _dot(hh, wrl_ref[...]) + _dot(hl, wrh_ref[...]) + br_ref[...]


def _merge_call(o_f, o_b, u, y_att, xin, x_row_off, B, T, mods, l, mod_row, wts, n_out_rows, prev_outs):
    tm = min(TM_MERGE, T)
    tpb = T // tm
    D = D_MODEL
    R = mods.shape[2]
    ro = x_row_off // tm
    ro16 = x_row_off // POOL_HALO
    r16 = tm // POOL_HALO
    n16 = u.shape[0] // POOL_HALO
    cgr, cpu = OFF_GR // 512, OFF_PU // 512
    n_seq_tiles = B * tpb
    n_grid = n_seq_tiles if prev_outs is not None else (n_out_rows - x_row_off) // tm
    fill_tail = n_grid > n_seq_tiles
    ic = lambda i: jnp.minimum(i, n_seq_tiles - 1)
    full = lambda shp: pl.BlockSpec(shp, lambda i: (0,) * len(shp))
    mod_spec = lambda k: pl.BlockSpec((None, None, R, D), lambda i: (l, k, 0, 0))
    in_specs = [
        pl.BlockSpec((tm, GLA_V), lambda i: (ic(i), 0)),
        pl.BlockSpec((tm, GLA_V), lambda i: (ic(i), 0)),
        pl.BlockSpec((tm, 512), lambda i: (ro + ic(i), cgr)),
        pl.BlockSpec((tm, 512), lambda i: (ro + ic(i), cpu)),
        pl.BlockSpec((POOL_HALO, 512), lambda i: (jnp.maximum(ro16 + ic(i) * r16 - 1, 0), cpu)),
        pl.BlockSpec((POOL_HALO, 512), lambda i: (jnp.minimum(ro16 + (ic(i) + 1) * r16, n16 - 1), cpu)),
        pl.BlockSpec((tm, N_BRANCH * D), lambda i: (ro + ic(i), 0)),
        pl.BlockSpec((tm, ATT_Q), lambda i: (ic(i), 0)),
        pl.BlockSpec((tm, D), lambda i: (ro + ic(i), 0)),
        mod_spec(2), mod_spec(3), mod_spec(4),
        full((1, GLA_DV)), full((4, 128, 128 + 2 * POOL_HALO)), full((4, POOL_GROUP, POOL_GROUP)), full((1, POOL_WIDTH)),
        full((N_BRANCH, BRANCH_WIDTH, D)), full((D, D)), full((1, D)), full((D, ROUTER_W)), full((D, ROUTER_W)),
        full((1, ROUTER_W)),
    ]
    args = [o_f, o_b, u, u, u, u, u, y_att, xin, mods, mods, mods, *wts]
    out_shape = [jax.ShapeDtypeStruct((n_out_rows, D), F32), jax.ShapeDtypeStruct((n_out_rows * TOK_ROWS, LANES), F32),
                 jax.ShapeDtypeStruct((n_out_rows, ROUTER_W), F32)]
    out_specs = [pl.BlockSpec((tm, D), lambda i: (ro + i, 0)), pl.BlockSpec((tm * TOK_ROWS, LANES), lambda i: (ro + i, 0)),
                 pl.BlockSpec((tm, ROUTER_W), lambda i: (ro + i, 0))]
    aliases = {}
    if prev_outs is not None:
        n_in = len(args)
        in_specs += [pl.BlockSpec(memory_space=pl.ANY)] * 3
        args += list(prev_outs)
        aliases = {n_in: 0, n_in + 1: 1, n_in + 2: 2}
    kern = functools.partial(_merge_kernel, n_seq_tiles=n_seq_tiles, fill_tail=fill_tail,
                             T=T, tm=tm, tiles_per_batch=tpb, row_base=mod_row)
    return pl.pallas_call(
        kern,
        grid=(n_grid,),
        in_specs=in_specs,
        out_specs=out_specs,
        out_shape=out_shape,
        input_output_aliases=aliases,
        compiler_params=_cparams(("arbitrary",)),
        name="merge",
    )(*args)


def _router_kernel(lg_ref, out_ref, cnt_ref, carry, pstart, *, tm, bm):
    ph = pl.program_id(0)
    i = pl.program_id(1)
    lane = lax.broadcasted_iota(jnp.int32, (1, LANES), 1)
    neg = -jnp.inf
    x = lg_ref[...]

    @pl.when((ph == 0) & (i == 0))
    def _():
        carry[...] = jnp.zeros_like(carry)

    gl = jnp.where(lane < N_GROUPS, x, neg)
    gmax = jnp.max(gl, axis=-1, keepdims=True)
    gsum = jnp.sum(jnp.exp(gl - gmax), axis=-1, keepdims=True)
    grp = jnp.min(jnp.where(gl == gmax, lane, LANES), axis=-1, keepdims=True)
    e_lane = lane - N_GROUPS
    lane_grp = sum((e_lane >= EXP_PER_GROUP * k).astype(jnp.int32) for k in range(1, N_GROUPS))
    lane_grp = jnp.where(e_lane < 0, -1, jnp.where(e_lane < N_EXPERTS, lane_grp, -1))
    in_grp = lane_grp == grp
    el = jnp.where(in_grp, x, neg)
    emax = jnp.max(el, axis=-1, keepdims=True)
    esum = jnp.sum(jnp.exp(el - emax), axis=-1, keepdims=True)
    i1 = jnp.min(jnp.where(el == emax, lane, LANES), axis=-1, keepdims=True)
    el2 = jnp.where(lane == i1, neg, el)
    m2 = jnp.max(el2, axis=-1, keepdims=True)
    i2 = jnp.min(jnp.where(el2 == m2, lane, LANES), axis=-1, keepdims=True)
    p1 = 1.0 / esum
    p2 = jnp.exp(m2 - emax) / esum
    pg = 1.0 / gsum
    w1 = pg * p1 / (p1 + p2)
    w2 = pg * p2 / (p1 + p2)
    oh1 = lane == i1
    oh2 = lane == i2
    oh = jnp.where(oh1, 1.0, jnp.where(oh2, 1.0, 0.0))

    @pl.when(ph == 0)
    def _():
        carry[...] += jnp.sum(oh, axis=0, keepdims=True)

    @pl.when((ph == 1) & (i == 0))
    def _():
        cnt = carry[...]
        cnt_ref[...] = jnp.broadcast_to(cnt, cnt_ref.shape)
        nb = jnp.floor((cnt + (bm - 1)) * (1.0 / bm))
        hi = jnp.floor(nb * (1.0 / 16.0))
        lo = nb - 16.0 * hi
        r = lax.broadcasted_iota(jnp.int32, (LANES, LANES), 0)
        c = lax.broadcasted_iota(jnp.int32, (LANES, LANES), 1)
        upper = jnp.where(r < c, 1.0, 0.0).astype(BF16)
        hi8 = jnp.broadcast_to(hi, (8, LANES)).astype(BF16)
        lo8 = jnp.broadcast_to(lo, (8, LANES)).astype(BF16)
        pre = 16.0 * _dot(hi8, upper) + _dot(lo8, upper)
        pstart[...] = pre[0:1, :] * bm
        carry[...] = jnp.zeros_like(carry)

    @pl.when(ph == 1)
    def _():
        r = lax.broadcasted_iota(jnp.int32, (tm, tm), 0)
        c = lax.broadcasted_iota(jnp.int32, (tm, tm), 1)
        lower = jnp.where(c < r, 1.0, 0.0).astype(BF16)
        before = _dot(lower, oh.astype(BF16))
        slot = pstart[...] + carry[...] + before
        d1 = jnp.sum(jnp.where(oh1, slot, 0.0), axis=-1, keepdims=True)
        d2 = jnp.sum(jnp.where(oh2, slot, 0.0), axis=-1, keepdims=True)
        carry[...] += jnp.sum(oh, axis=0, keepdims=True)
        out_ref[...] = jnp.where(lane == 0, d1, jnp.where(lane == 1, d2, jnp.where(lane == 2, w1, jnp.where(lane == 3, w2, 0.0))))


def _router_call(logits, bm):
    N = logits.shape[0]
    tm = TM_ROUTE
    return pl.pallas_call(
        functools.partial(_router_kernel, tm=tm, bm=bm),
        grid=(2, N // tm),
        in_specs=[pl.BlockSpec((tm, ROUTER_W), lambda p, i: (i, 0))],
        out_specs=[pl.BlockSpec((tm, LANES), lambda p, i: (i * p, 0)), pl.BlockSpec((8, LANES), lambda p, i: (0, 0))],
        out_shape=[jax.ShapeDtypeStruct((N, LANES), F32), jax.ShapeDtypeStruct((8, LANES), F32)],
        scratch_shapes=[pltpu.VMEM((1, LANES), F32), pltpu.VMEM((1, LANES), F32)],
        compiler_params=_cparams(("arbitrary", "arbitrary")),
        name="router",
    )(logits)


def _tok_rows(t):
    return pl.ds(pl.multiple_of(t * TOK_ROWS, TOK_ROWS), TOK_ROWS)


def _dispatch_kernel(idx_hbm, h2_ref, xs_in, xs_hbm, idx_s, isem, dsem, *, tm, n_tiles):
    del xs_in
    i = pl.program_id(0)
    s = i % 2

    def idx_copy(j):
        return pltpu.make_async_copy(idx_hbm.at[j], idx_s.at[j % 2], isem.at[j % 2])

    def row_copy(r, slot):
        return pltpu.make_async_copy(h2_ref.at[_tok_rows(r), :], xs_hbm.at[_tok_rows(slot), :], dsem)

    @pl.when(i == 0)
    def _():
        idx_copy(0).start()

    @pl.when(i + 1 < n_tiles)
    def _():
        idx_copy(i + 1).start()

    idx_copy(i).wait()

    def issue(r0, c):
        for u in range(DMA_UNROLL):
            r = r0 * DMA_UNROLL + u
            for k in range(TOP_K):
                row_copy(r, idx_s[s, 0, TOP_K * r + k]).start()
        return c

    lax.fori_loop(0, tm // DMA_UNROLL, issue, 0)

    def drain(r0, c):
        for _ in range(TOP_K * DMA_UNROLL):
            row_copy(0, 0).wait()
        return c

    lax.fori_loop(0, tm // DMA_UNROLL, drain, 0)


def _dispatch_call(idx, h2, n_slots):
    n_tiles = idx.shape[0]
    tm = TM_DISP
    xs0 = jnp.zeros((n_slots * TOK_ROWS, LANES), F32)
    return pl.pallas_call(
        functools.partial(_dispatch_kernel, tm=tm, n_tiles=n_tiles),
        grid=(n_tiles,),
        in_specs=[pl.BlockSpec(memory_space=pl.ANY),
                  pl.BlockSpec((tm * TOK_ROWS, LANES), lambda i: (i, 0)),
                  pl.BlockSpec(memory_space=pl.ANY)],
        out_specs=pl.BlockSpec(memory_space=pl.ANY),
        out_shape=jax.ShapeDtypeStruct((n_slots * TOK_ROWS, LANES), F32),
        input_output_aliases={2: 0},
        scratch_shapes=[pltpu.SMEM((2, 1, TOP_K * tm), jnp.int32), pltpu.SemaphoreType.DMA((2,)),
                        pltpu.SemaphoreType.DMA(())],
        compiler_params=pltpu.CompilerParams(dimension_semantics=("arbitrary",), vmem_limit_bytes=VMEM_LIMIT_BYTES,
                                             has_side_effects=True),
        name="dispatch",
    )(idx, h2, xs0)


def _expert_kernel(be_ref, nu_ref, xs_ref, wg_ref, wu_ref, wd_ref, ys_ref, *, bm):
    i = pl.program_id(0)

    @pl.when(i < nu_ref[0])
    def _():
        xb = _load_token_rows(xs_ref, bm).astype(BF16)
        g = _dot(xb, wg_ref[...])
        up = _dot(xb, wu_ref[...])
        hmid = (_silu(g) * up).astype(BF16)
        _store_token_rows(ys_ref, _dot(hmid, wd_ref[...]))

    @pl.when(i >= nu_ref[0])
    def _():
        ys_ref[...] = jnp.zeros_like(ys_ref)


def _expert_call(blk_expert, nused, xs, wg, wu, wd):
    bm = BM_MOE
    D = D_MODEL
    nblk = blk_expert.shape[0]
    w_in_spec = pl.BlockSpec((None, D, D_EXPERT), lambda i, be, nu: (be[i], 0, 0))
    grid_spec = pltpu.PrefetchScalarGridSpec(
        num_scalar_prefetch=2,
        grid=(nblk,),
        in_specs=[
            pl.BlockSpec((bm * TOK_ROWS, LANES), lambda i, be, nu: (jnp.minimum(i, nu[0] - 1), 0)),
            w_in_spec, w_in_spec,
            pl.BlockSpec((None, D_EXPERT, D), lambda i, be, nu: (be[i], 0, 0)),
        ],
        out_specs=pl.BlockSpec((bm * TOK_ROWS, LANES), lambda i, be, nu: (i, 0)),
    )
    return pl.pallas_call(
        functools.partial(_expert_kernel, bm=bm),
        grid_spec=grid_spec,
        out_shape=jax.ShapeDtypeStruct(xs.shape, F32),
        compiler_params=_cparams(("arbitrary",)),
        name="experts",
    )(blk_expert, nused, xs, wg, wu, wd)


def _combine_kernel(idx_hbm, ys_hbm, x_ref, r_ref, g2_ref, fg_ref, o_ref, idx_s, isem, ybuf, gsem, *,
                    tm, n_tiles, n_lat_tiles, tiles_per_batch, ctx_row, final):
    i = pl.program_id(0)

    def idx_copy(j):
        return pltpu.make_async_copy(idx_hbm.at[j], idx_s.at[j % 3], isem.at[j % 3])

    def row_copy(j, r, k, slot):
        return pltpu.make_async_copy(ys_hbm.at[_tok_rows(slot), :], ybuf.at[j % 2, _tok_rows(k * tm + r), :],
                                     gsem.at[j % 2])

    def start_gather(j):
        def issue(r0, c):
            for u in range(DMA_UNROLL):
                r = r0 * DMA_UNROLL + u
                for k in range(TOP_K):
                    row_copy(j, r, k, idx_s[j % 3, 0, TOP_K * r + k]).start()
            return c
        lax.fori_loop(0, tm // DMA_UNROLL, issue, 0)

    def wait_gather(j):
        def drain(r0, c):
            for _ in range(TOP_K * DMA_UNROLL):
                row_copy(j, 0, 0, 0).wait()
            return c
        lax.fori_loop(0, tm // DMA_UNROLL, drain, 0)

    @pl.when(i == 0)
    def _():
        idx_copy(0).start()
        idx_copy(0).wait()
        start_gather(0)
        if n_tiles > 1:
            idx_copy(1).start()

    @pl.when(i + 1 < n_tiles)
    def _():
        idx_copy(i + 1).wait()
        start_gather(i + 1)

    @pl.when(i + 2 < n_tiles)
    def _():
        idx_copy(i + 2).start()

    wait_gather(i)
    yb = ybuf.at[i % 2]
    y0 = _load_token_rows(yb, tm, 0)
    y1 = _load_token_rows(yb, tm, tm * TOK_ROWS)
    m = r_ref[:, 2:3] * y0 + r_ref[:, 3:4] * y1
    b = jnp.where(i < n_lat_tiles, i // tiles_per_batch, ctx_row)
    xn = x_ref[...] + g2_ref[pl.ds(b, 1), :] * m
    if final:
        ms = jnp.mean(xn * xn, axis=-1, keepdims=True)
        xn = xn * lax.rsqrt(ms + NORM_EPS) * fg_ref[...]
    o_ref[...] = xn


def _combine_call(idx, ys, x_mid, route, mods, l, final_g, n_lat_rows, rows_per_batch, ctx_row, final):
    tm = TM_COMB
    D = D_MODEL
    R = mods.shape[2]
    n_rows = x_mid.shape[0]
    n_tiles = n_rows // tm
    kern = functools.partial(_combine_kernel, tm=tm, n_tiles=n_tiles, n_lat_tiles=n_lat_rows // tm,
                             tiles_per_batch=rows_per_batch // tm, ctx_row=ctx_row, final=final)
    return pl.pallas_call(
        kern,
        grid=(n_tiles,),
        in_specs=[
            pl.BlockSpec(memory_space=pl.ANY),
            pl.BlockSpec(memory_space=pl.ANY),
            pl.BlockSpec((tm, D), lambda i: (i, 0)),
            pl.BlockSpec((tm, LANES), lambda i: (i, 0)),
            pl.BlockSpec((None, None, R, D), lambda i: (l, 5, 0, 0)),
            pl.BlockSpec((1, D), lambda i: (0, 0)),
        ],
        out_specs=pl.BlockSpec((tm, D), lambda i: (i, 0)),
        out_shape=jax.ShapeDtypeStruct((n_rows, D), F32),
        scratch_shapes=[pltpu.SMEM((3, 1, TOP_K * tm), jnp.int32), pltpu.SemaphoreType.DMA((3,)),
                        pltpu.VMEM((2, TOP_K * tm * TOK_ROWS, LANES), F32), pltpu.SemaphoreType.DMA((2,))],
        compiler_params=_cparams(("arbitrary",)),
        name="combine",
    )(idx, ys, x_mid, route, mods, final_g.reshape(1, D))


def _permute_w_in(w_in):
    gq, gk, gv, gr, glf, glb, aq, ak, av, pu, gt = jnp.split(w_in, np.cumsum(IN_SIZES)[:-1].tolist(), axis=-1)
    pad = jnp.zeros(w_in.shape[:-1] + (U_WIDTH - sum(IN_SIZES),), w_in.dtype)
    return jnp.concatenate([gt, gq, gk, gv, gr, aq, pu, ak, av, glf, glb, pad], axis=-1).astype(BF16)


def _rope_tables(T):
    rows = T // GRID_W
    row = np.repeat(np.arange(rows), GRID_W).astype(np.float32)
    col = np.tile(np.arange(GRID_W), rows).astype(np.float32)
    inv = jnp.asarray(ROPE_THETA, F32) ** (-jnp.arange(0, ROPE_AXIS_DIM, 2, dtype=F32) / ROPE_AXIS_DIM)
    ang_r = jnp.asarray(row)[:, None] * inv
    ang_c = jnp.asarray(col)[:, None] * inv
    zero = jnp.zeros_like(ang_r)
    cos = jnp.concatenate([jnp.cos(ang_r)] * 2 + [jnp.cos(ang_c)] * 2, axis=1)
    s1 = jnp.concatenate([-jnp.sin(ang_r), zero, -jnp.sin(ang_c), zero], axis=1)
    s2 = jnp.concatenate([zero, jnp.sin(ang_r), zero, jnp.sin(ang_c)], axis=1)
    return tuple(jnp.concatenate([t, t], axis=1) for t in (cos, s1, s2))


def _pool_bands():
    i = np.arange(128)[:, None]
    j = np.arange(128 + 2 * POOL_HALO)[None, :]
    bands = [((j >= i + POOL_HALO - w // 2) & (j < i + POOL_HALO + w // 2)).astype(np.float32) for w in POOL_WINDOWS]
    return jnp.asarray(np.stack(bands), BF16)


def _block_table(counts, n_tok, bm):
    cnt = counts[0, N_GROUPS:N_GROUPS + N_EXPERTS].astype(jnp.int32)
    pad_end = jnp.cumsum((cnt + bm - 1) // bm * bm)
    nblk = -(-(n_tok * TOP_K + N_EXPERTS * (bm - 1)) // bm)
    blk_start = jnp.arange(nblk, dtype=jnp.int32) * bm
    blk_expert = jnp.minimum(jnp.sum((pad_end[None, :] <= blk_start[:, None]).astype(jnp.int32), axis=1), N_EXPERTS - 1)
    nused = (pad_end[-1] // bm).astype(jnp.int32).reshape(1)
    return blk_expert.astype(jnp.int32), nused, nblk


def _slot_tiles(route, tm):
    n = route.shape[0]
    return route[:, :TOP_K].astype(jnp.int32).reshape(n // tm, 1, TOP_K * tm)


def kernel(x, c, ctx, c_ctx, w_mod, b_mod, norm1_g, norm2_g, w_in, gla_a_up_f, gla_a_bias_f, gla_a_up_b, gla_a_bias_b, gla_norm_g, att_qn_g, att_kn_g, pool_w, pool_scale, w_branch, w_out, moe_w_group, moe_b_group, moe_w_expert, moe_b_expert, moe_w_gate, moe_w_up, moe_w_down, final_g):
    B, T, D = x.shape
    C = ctx.shape[1]
    L = w_mod.shape[0]
    n_lat, n_ctx = B * T, B * C
    MOD_ROWS = 16
    assert D == D_MODEL and B < MOD_ROWS and T % TQ_ATT == 0 and C % GLA_CHUNK == 0

    s_in = jnp.concatenate([c, c_ctx[None], jnp.zeros((MOD_ROWS - B - 1, D), F32)], axis=0)
    mods = _mod_call(s_in, w_mod, b_mod)
    w_perm = _permute_w_in(w_in)
    tabs = _rope_tables(T)
    bands = _pool_bands()
    zero_state = jnp.zeros((B, GLA_V, GLA_QK), F32)

    xall = jnp.concatenate([x.reshape(n_lat, D), ctx.reshape(n_ctx, D)], axis=0)
    out = None
    for l in range(L):
        want_ctx = l < L - 1
        n_rows = xall.shape[0]
        u = _inproj_call(xall, mods, l, norm1_g[l], w_perm[l], n_lat, T, B)

        up = jnp.zeros((128, 2 * GLA_QK), F32)
        up = up.at[:GLA_RANK, :GLA_QK].set(gla_a_up_f[l]).at[GLA_RANK:2 * GLA_RANK, GLA_QK:].set(gla_a_up_b[l]).astype(BF16)
        bias = jnp.concatenate([gla_a_bias_f[l], gla_a_bias_b[l]]).reshape(1, 2 * GLA_QK)
        ofc, obc, sfc, sbc = _gla_call(u, n_lat, B, C, zero_state, zero_state, up, bias)
        of, ob, _, _ = _gla_call(u, 0, B, T, sfc, sbc, up, bias)

        qg = jnp.tile(att_qn_g[l], 2).reshape(1, LANES)
        kg = jnp.tile(att_kn_g[l], 2).reshape(1, LANES)
        ya = _att_call(u, 0, B, T, T, C, n_lat, tabs, qg, kg)

        wr = jnp.zeros((D, ROUTER_W), F32).at[:, :N_GROUPS].set(moe_w_group[l]).at[:, N_GROUPS:N_GROUPS + N_EXPERTS].set(moe_w_expert[l])
        wrh, wrl = _split_bf16(wr)
        br = jnp.zeros((1, ROUTER_W), F32).at[0, :N_GROUPS].set(moe_b_group[l]).at[0, N_GROUPS:N_GROUPS + N_EXPERTS].set(moe_b_expert[l])
        wts = (gla_norm_g[l].reshape(1, GLA_DV), bands, pool_w[l].astype(BF16), pool_scale[l].reshape(1, POOL_WIDTH),
               w_branch[l].astype(BF16), w_out[l].astype(BF16), norm2_g[l].reshape(1, D), wrh, wrl, br)
        n_tok = n_rows if want_ctx else n_lat
        outs = _merge_call(of, ob, u, ya, xall, 0, B, T, mods, l, None, wts, n_tok, None)
        if want_ctx:
            yac = _att_call(u, n_lat, B, C, 0, C, n_lat, None, qg, kg)
            outs = _merge_call(ofc, obc, u, yac, xall, n_lat, B, C, mods, l, B, wts, n_tok, outs)
        x_mid, h2, logits = outs

        route, counts = _router_call(logits, BM_MOE)
        blk_expert, nused, nblk = _block_table(counts, n_tok, BM_MOE)
        xs = _dispatch_call(_slot_tiles(route, TM_DISP), h2, nblk * BM_MOE)
        ys = _expert_call(blk_expert, nused, xs, moe_w_gate[l].astype(BF16), moe_w_up[l].astype(BF16),
                          moe_w_down[l].astype(BF16))
        xall = _combine_call(_slot_tiles(route, TM_COMB), ys, x_mid, route, mods, l, final_g, n_lat, T, B,
                             final=not want_ctx)
    return xall[:n_lat].reshape(B, T, D)
```

```python
import functools

import numpy as np
import jax
import jax.numpy as jnp
from jax import lax
from jax.experimental import pallas as pl
from jax.experimental.pallas import tpu as pltpu

F32 = jnp.float32
BF16 = jnp.bfloat16

VMEM_LIMIT_BYTES = 56 * 1024 * 1024
LANES = 128

D_MODEL = 1024
GRID_W = 64
NORM_EPS = 1e-6
GLA_HEADS, GLA_DK, GLA_DV, GLA_RANK, GLA_TAU, GLA_CHUNK = 4, 64, 128, 16, 16.0, 64
GLA_QK, GLA_V = GLA_HEADS * GLA_DK, GLA_HEADS * GLA_DV
ATT_HEADS, ATT_KV_HEADS, ATT_DH = 8, 2, 64
ROPE_THETA, ROPE_AXIS_DIM = 10000.0, 32
ATT_Q, ATT_KV = ATT_HEADS * ATT_DH, ATT_KV_HEADS * ATT_DH
POOL_WINDOWS, POOL_GROUP = (2, 4, 8, 16), 128
POOL_WIDTH = POOL_GROUP * len(POOL_WINDOWS)
POOL_HALO = 16
N_BRANCH, BRANCH_WIDTH = 3, 512
N_GROUPS, EXP_PER_GROUP, TOP_K, D_EXPERT = 4, 8, 2, 512
N_EXPERTS = N_GROUPS * EXP_PER_GROUP
IN_SIZES = (GLA_QK, GLA_QK, GLA_V, GLA_V, GLA_RANK, GLA_RANK, ATT_Q, ATT_KV, ATT_KV, POOL_WIDTH, N_BRANCH * D_MODEL)

OFF_GT, OFF_GQK, OFF_GV, OFF_GR, OFF_AQ, OFF_PU, OFF_AKV, OFF_GL = 0, 3072, 3584, 4096, 4608, 5120, 5632, 5888
U_WIDTH = 6144
U_CHUNK = 512

TM_IN = 512
TB_GLA = 256
GLA_CHUNKS_PER_BODY = 4
TQ_ATT = 512
TK_ATT = 512
ATT_SUB_ROWS = 64
LOG2_E = 1.4426950408889634
ATT_FLAGS = {}
ATT_V_ROWS = 80
ATT_SCORE_LOOKAHEAD = 2
TM_MERGE = 512
BM_MOE = 256
TM_ROUTE = 512
TM_DISP = 256
TM_COMB = 256
ROUTER_W = 128
TOK_ROWS = D_MODEL // LANES
DMA_UNROLL = 8


def _cparams(sem):
    return pltpu.CompilerParams(dimension_semantics=sem, vmem_limit_bytes=VMEM_LIMIT_BYTES)


def _split_bf16(a):
    hi = a.astype(BF16)
    lo = (a - hi.astype(F32)).astype(BF16)
    return hi, lo


def _dot(a, b):
    return jnp.dot(a, b, preferred_element_type=F32)


def _dot_nt(a, b):
    return lax.dot_general(a, b, (((1,), (1,)), ((), ())), preferred_element_type=F32)


def _dot_tn(a, b):
    return lax.dot_general(a, b, (((0,), (0,)), ((), ())), preferred_element_type=F32)


def _dot3(a, b):
    ah, al = _split_bf16(a)
    bh, bl = _split_bf16(b)
    return _dot(ah, bh) + _dot(ah, bl) + _dot(al, bh)


def _load_token_rows(ref, n_tok, row0=0):
    return jnp.concatenate([ref[pl.ds(row0 + s, n_tok, stride=TOK_ROWS), :] for s in range(TOK_ROWS)], axis=1)


def _store_token_rows(ref, val):
    n_tok = val.shape[0]
    for s in range(TOK_ROWS):
        ref[pl.ds(s, n_tok, stride=TOK_ROWS), :] = val[:, s * LANES:(s + 1) * LANES]


def _sigmoid(x):
    return 1.0 / (1.0 + jnp.exp(-x))


def _silu(x):
    return x * _sigmoid(x)


def _mod_kernel(s_ref, w_ref, b_ref, o_ref):
    s = _silu(s_ref[...])
    o_ref[...] = _dot3(s, w_ref[...]) + b_ref[...]


def _mod_call(s_in, w_mod, b_mod):
    L, D, _ = w_mod.shape
    R = s_in.shape[0]
    return pl.pallas_call(
        _mod_kernel,
        grid=(L, 6),
        in_specs=[
            pl.BlockSpec((R, D), lambda l, j: (0, 0)),
            pl.BlockSpec((None, D, D), lambda l, j: (l, 0, j)),
            pl.BlockSpec((None, None, 1, D), lambda l, j: (l, j, 0, 0)),
        ],
        out_specs=pl.BlockSpec((None, None, R, D), lambda l, j: (l, j, 0, 0)),
        out_shape=jax.ShapeDtypeStruct((L, 6, R, D), F32),
        compiler_params=_cparams(("arbitrary", "arbitrary")),
        name="mod_table",
    )(s_in, w_mod, b_mod.reshape(L, 6, 1, D))


def _inproj_kernel(x_ref, sh_ref, sc_ref, g_ref, w_ref, o_ref, *, n_lat_tiles, tiles_per_batch, ctx_row):
    i = pl.program_id(0)
    b = jnp.where(i < n_lat_tiles, i // tiles_per_batch, ctx_row)
    x = x_ref[...]
    ms = jnp.mean(x * x, axis=-1, keepdims=True)
    h = x * lax.rsqrt(ms + NORM_EPS) * g_ref[...]
    h = h * (1.0 + sc_ref[pl.ds(b, 1), :]) + sh_ref[pl.ds(b, 1), :]
    hb = h.astype(BF16)
    for c in range(U_WIDTH // U_CHUNK):
        cs = slice(c * U_CHUNK, (c + 1) * U_CHUNK)
        o_ref[:, cs] = _dot(hb, w_ref[:, cs]).astype(BF16)


def _inproj_call(x, mods, l, norm_g, w_perm, n_lat_rows, rows_per_batch, ctx_row):
    N, D = x.shape
    R = mods.shape[2]
    tm = TM_IN
    kern = functools.partial(_inproj_kernel, n_lat_tiles=n_lat_rows // tm,
                             tiles_per_batch=rows_per_batch // tm, ctx_row=ctx_row)
    return pl.pallas_call(
        kern,
        grid=(N // tm,),
        in_specs=[
            pl.BlockSpec((tm, D), lambda i: (i, 0)),
            pl.BlockSpec((None, None, R, D), lambda i: (l, 0, 0, 0)),
            pl.BlockSpec((None, None, R, D), lambda i: (l, 1, 0, 0)),
            pl.BlockSpec((1, D), lambda i: (0, 0)),
            pl.BlockSpec((D, U_WIDTH), lambda i: (0, 0)),
        ],
        out_specs=pl.BlockSpec((tm, U_WIDTH), lambda i: (i, 0)),
        out_shape=jax.ShapeDtypeStruct((N, U_WIDTH), BF16),
        compiler_params=_cparams(("arbitrary",)),
        name="inproj",
    )(x, mods, mods, norm_g.reshape(1, D), w_perm)


def _gla_chunks(work):
    Lc = GLA_CHUNK
    r = lax.broadcasted_iota(jnp.int32, (Lc, Lc), 0)
    c = lax.broadcasted_iota(jnp.int32, (Lc, Lc), 1)
    lane_head = lax.broadcasted_iota(jnp.int32, (Lc, GLA_QK), 1) // GLA_DK
    vlane_head = lax.broadcasted_iota(jnp.int32, (Lc, GLA_V), 1) // GLA_DV
    t_i = lax.broadcasted_iota(jnp.int32, (Lc, GLA_HEADS * Lc), 0)
    s_i = lax.broadcasted_iota(jnp.int32, (Lc, GLA_HEADS * Lc), 1) % Lc
    row_head = lax.broadcasted_iota(jnp.int32, (GLA_V, GLA_QK), 0) // GLA_DV
    col_head = lax.broadcasted_iota(jnp.int32, (GLA_V, GLA_QK), 1) // GLA_DK

    zs = [_dot(gl, up) + bias for (_, _, gl, up, bias, _, _) in work]
    bs = []
    for z, (_, _, _, _, _, _, forward) in zip(zs, work):
        la = (jnp.minimum(z, 0.0) - jnp.log(1.0 + jnp.exp(-jnp.abs(z)))) * (1.0 / GLA_TAU)
        tri = jnp.where((r >= c) if forward else (r <= c), 1.0, 0.0).astype(BF16)
        la_hi, la_lo = _split_bf16(la)
        bs.append(_dot(tri, la_hi) + _dot(tri, la_lo))
    mids = []
    for b, (qk, v, _, _, _, _, forward) in zip(bs, work):
        q = qk[:, :GLA_QK].astype(F32) * (GLA_DK ** -0.5)
        k = qk[:, GLA_QK:].astype(F32)
        i_last, i_ref = (Lc - 1, Lc // 2) if forward else (0, Lc - 1 - Lc // 2)
        b_last = b[i_last:i_last + 1, :]
        b_ref = b[i_ref:i_ref + 1, :]
        qd = (q * jnp.exp(b - b_ref)).astype(BF16)
        kd = (k * jnp.exp(b_ref - b)).astype(BF16)
        qe = (q * jnp.exp(b)).astype(BF16)
        kl = (k * jnp.exp(b_last - b)).astype(BF16)
        a = jnp.exp(b_last)
        kd_blk = jnp.concatenate([jnp.where(lane_head == h, kd, jnp.zeros_like(kd)) for h in range(GLA_HEADS)], axis=0)
        v_blk = jnp.concatenate([jnp.where(vlane_head == h, v, jnp.zeros_like(v)) for h in range(GLA_HEADS)], axis=0)
        mids.append((qd, kd_blk, qe, kl, a, v_blk))
    scs = [_dot_nt(qd, kd_blk) for (qd, kd_blk, _, _, _, _) in mids]
    dss = [_dot_tn(v, kl) for (_, v, _, _, _, _, _), (_, _, _, kl, _, _) in zip(work, mids)]
    outs = []
    for sc, ds, (_, _, qe, _, a, v_blk), (_, _, _, _, _, st_ref, forward) in zip(scs, dss, mids, work):
        sc = jnp.where((s_i <= t_i) if forward else (s_i >= t_i), sc, 0.0).astype(BF16)
        st = st_ref[...]
        outs.append(_dot(sc, v_blk) + _dot_nt(qe, st.astype(BF16)))
        st_ref[...] = a * st + jnp.where(row_head == col_head, ds, 0.0)
    return outs


def _gla_kernel(qkf_ref, vf_ref, glf_ref, qkb_ref, vb_ref, glb_ref, up_ref, bias_ref, s0f_ref, s0b_ref,
                of_ref, ob_ref, sf_ref, sb_ref, stf, stb, *, nsub):
    n = pl.program_id(1)

    @pl.when(n == 0)
    def _():
        stf[...] = s0f_ref[...]
        stb[...] = s0b_ref[...]

    def body(j, carry):
        work, rows = [], []
        for u in range(GLA_CHUNKS_PER_BODY):
            jj = j * GLA_CHUNKS_PER_BODY + u
            rf = pl.ds(pl.multiple_of(jj * GLA_CHUNK, GLA_CHUNK), GLA_CHUNK)
            rb = pl.ds(pl.multiple_of((nsub - 1 - jj) * GLA_CHUNK, GLA_CHUNK), GLA_CHUNK)
            work += [
                (qkf_ref[rf, :], vf_ref[rf, :], glf_ref[rf, :], up_ref[:, :GLA_QK], bias_ref[:, :GLA_QK], stf, True),
                (qkb_ref[rb, :], vb_ref[rb, :], glb_ref[rb, :], up_ref[:, GLA_QK:], bias_ref[:, GLA_QK:], stb, False)]
            rows += [(of_ref, rf), (ob_ref, rb)]
        for (ref, rr), o in zip(rows, _gla_chunks(work)):
            ref[rr, :] = o
        return carry

    lax.fori_loop(0, nsub // GLA_CHUNKS_PER_BODY, body, 0)

    @pl.when(n == pl.num_programs(1) - 1)
    def _():
        sf_ref[...] = stf[...]
        sb_ref[...] = stb[...]


def _gla_call(u, row_off, B, T, s0f, s0b, up, bias):
    tb = min(TB_GLA, T)
    nb = T // tb
    off = row_off // tb
    fwd = lambda b, n: off + b * nb + n
    bwd = lambda b, n: off + b * nb + (nb - 1 - n)
    cqk, cv, cgl = OFF_GQK // 512, OFF_GV // 512, OFF_GL // 128
    st_spec = pl.BlockSpec((None, GLA_V, GLA_QK), lambda b, n: (b, 0, 0))
    st_shape = jax.ShapeDtypeStruct((B, GLA_V, GLA_QK), F32)
    o_shape = jax.ShapeDtypeStruct((B * T, GLA_V), F32)
    return pl.pallas_call(
        functools.partial(_gla_kernel, nsub=tb // GLA_CHUNK),
        grid=(B, nb),
        in_specs=[
            pl.BlockSpec((tb, 512), lambda b, n: (fwd(b, n), cqk)),
            pl.BlockSpec((tb, 512), lambda b, n: (fwd(b, n), cv)),
            pl.BlockSpec((tb, 128), lambda b, n: (fwd(b, n), cgl)),
            pl.BlockSpec((tb, 512), lambda b, n: (bwd(b, n), cqk)),
            pl.BlockSpec((tb, 512), lambda b, n: (bwd(b, n), cv)),
            pl.BlockSpec((tb, 128), lambda b, n: (bwd(b, n), cgl)),
            pl.BlockSpec((128, 2 * GLA_QK), lambda b, n: (0, 0)),
            pl.BlockSpec((1, 2 * GLA_QK), lambda b, n: (0, 0)),
            st_spec, st_spec,
        ],
        out_specs=[
            pl.BlockSpec((tb, GLA_V), lambda b, n: (b * nb + n, 0)),
            pl.BlockSpec((tb, GLA_V), lambda b, n: (b * nb + (nb - 1 - n), 0)),
            st_spec, st_spec,
        ],
        out_shape=[o_shape, o_shape, st_shape, st_shape],
        scratch_shapes=[pltpu.VMEM((GLA_V, GLA_QK), F32), pltpu.VMEM((GLA_V, GLA_QK), F32)],
        compiler_params=_cparams(("arbitrary", "arbitrary")),
        name="gla_scan",
    )(u, u, u, u, u, u, up, bias, s0f, s0b)


def _group_ms64(x):
    i = lax.broadcasted_iota(jnp.int32, (LANES, LANES), 0) // ATT_DH
    j = lax.broadcasted_iota(jnp.int32, (LANES, LANES), 1) // ATT_DH
    bd = jnp.where(i == j, 1.0, 0.0).astype(BF16)
    hi, lo = _split_bf16(x * x)
    return (_dot(hi, bd) + _dot(lo, bd)) * (1.0 / ATT_DH)


def _rope128(x, cos, s1, s2):
    return x * cos + pltpu.roll(x, LANES - ROPE_AXIS_DIM // 2, axis=1) * s1 + pltpu.roll(x, ROPE_AXIS_DIM // 2, axis=1) * s2


def _att_kernel(*refs, T, C, tk, use_rope):
    kx, vx, qh, sc_lead = refs[-4:]
    if T:
        (q_ref, kvl_ref, kvc_ref, cq_ref, s1q_ref, s2q_ref, ck_ref, s1k_ref, s2k_ref, qg_ref, kg_ref,
         o_ref) = refs[:-4]
    else:
        q_ref, kvc_ref, qg_ref, kg_ref, o_ref = refs[:-4]
    i = pl.program_id(1)
    lo_half = lax.broadcasted_iota(jnp.int32, (1, LANES), 1) < ATT_DH

    def put_kv(rows, kv, rope_tabs):
        k = kv[:, :LANES].astype(F32)
        v = kv[:, LANES:].astype(F32)
        k = k * lax.rsqrt(_group_ms64(k) + NORM_EPS) * kg_ref[...]
        if rope_tabs is not None:
            k = _rope128(k, *rope_tabs)
        k_sw = pltpu.roll(k, ATT_DH, axis=1)
        v_sw = pltpu.roll(v, ATT_DH, axis=1)
        kx[0, rows, :] = jnp.where(lo_half, k, k_sw).astype(BF16)
        kx[1, rows, :] = jnp.where(lo_half, k_sw, k).astype(BF16)
        rid = lax.broadcasted_iota(jnp.int32, (ATT_V_ROWS, 1), 0)
        for g, vg in enumerate((v, v_sw)):
            vt = vg.T[:ATT_V_ROWS, :]
            vx[g, :, rows] = jnp.where(rid < ATT_DH, vt, jnp.where(rid == ATT_DH, 1.0, 0.0)).astype(BF16)

    @pl.when(i == 0)
    def _():
        if T:
            def body(j, carry):
                rows = pl.ds(pl.multiple_of(j * tk, tk), tk)
                put_kv(rows, kvl_ref[rows, :], (ck_ref[rows, :], s1k_ref[rows, :], s2k_ref[rows, :]))
                return carry
            lax.fori_loop(0, T // tk, body, 0)
        put_kv(pl.ds(T, C), kvc_ref[...], None)

    n_slab = ATT_Q // LANES
    tq = q_ref.shape[0]
    for s in range(n_slab):
        q = q_ref[:, s * LANES:(s + 1) * LANES].astype(F32)
        q = q * lax.rsqrt(_group_ms64(q) + NORM_EPS) * qg_ref[...]
        if use_rope:
            q = _rope128(q, cq_ref[...], s1q_ref[...], s2q_ref[...])
        q = (q * (ATT_DH ** -0.5 * LOG2_E)).astype(BF16)
        qh[2 * s] = jnp.where(lo_half, q, jnp.zeros_like(q))
        qh[2 * s + 1] = jnp.where(lo_half, jnp.zeros_like(q), q)

    heads_per_kv = ATT_HEADS // ATT_KV_HEADS
    lo_rows = lax.broadcasted_iota(jnp.int32, (LANES, 1), 0) < ATT_DH
    for g in range(ATT_KV_HEADS):
        heads = range(g * heads_per_kv, (g + 1) * heads_per_kv)

        ahead = ATT_SCORE_LOOKAHEAD

        def chunk(rows, width, carry, lead_ready, next_rows):
            kc = kx[g, rows, :]
            vt = vx[g, :, rows]
            scores = {}
            if not lead_ready:
                scores = {h: _dot_nt(kc, qh[h]) for h in heads[:ahead]}
            new = []
            for n, (h, (m, acc)) in enumerate(zip(heads, carry)):
                sc = scores.pop(h) if h in scores else sc_lead[n, :width, :]
                if n + ahead < len(heads):
                    scores[heads[n + ahead]] = _dot_nt(kc, qh[heads[n + ahead]])
                elif next_rows is not None:
                    sc_lead[n + ahead - len(heads)] = _dot_nt(kx[g, next_rows, :], qh[heads[n + ahead - len(heads)]])
                m_new = jnp.maximum(m, jnp.max(sc, axis=0, keepdims=True))
                p = jnp.exp2(sc - m_new).astype(BF16)
                acc = jnp.exp2(m - m_new) * acc + _dot(vt, p)
                new.append((m_new, acc))
            return tuple(new)

        init = (jnp.full((1, tq), -jnp.inf, F32), jnp.zeros((ATT_V_ROWS, tq), F32))
        carry = (init,) * heads_per_kv
        key_rows = lambda j: pl.ds(j * tk if isinstance(j, int) else pl.multiple_of(j * tk, tk), tk)
        if T:
            n_blk = T // tk
            carry = chunk(pl.ds(T, C), C, carry, False, key_rows(0))
            carry = lax.fori_loop(
                0, n_blk, lambda j, cr: chunk(key_rows(j), tk, cr, True, key_rows(jnp.minimum(j + 1, n_blk - 1))), carry)
        else:
            carry = chunk(pl.ds(T, C), C, carry, False, None)
        for hi in range(0, heads_per_kv, 2):
            s = (g * heads_per_kv + hi) // 2
            halves = [a[:ATT_DH, :] / a[ATT_DH:ATT_DH + 1, :] for _, a in (carry[hi], carry[hi + 1])]
            o_ref[:, s * LANES:(s + 1) * LANES] = jnp.concatenate(halves, axis=0).T.astype(BF16)


def _att_call(u, q_row_off, B, Tq, T, C, ctx_row_off, tabs, qg, kg):
    tq = min(TQ_ATT, Tq)
    nq = Tq // tq
    tk = TK_ATT
    S = T + C
    hpk = ATT_HEADS // ATT_KV_HEADS
    cq, ckv = OFF_AQ // 512, OFF_AKV // 256
    qoff = q_row_off // tq
    coff = ctx_row_off // C
    g_spec = pl.BlockSpec((1, LANES), lambda b, i: (0, 0))
    q_spec = pl.BlockSpec((tq, ATT_Q), lambda b, i: (qoff + b * nq + i, cq))
    kvc_spec = pl.BlockSpec((C, 2 * ATT_KV), lambda b, i: (coff + b, ckv))
    if T:
        tq_tab = pl.BlockSpec((tq, LANES), lambda b, i: (i, 0))
        tk_tab = pl.BlockSpec((T, LANES), lambda b, i: (0, 0))
        in_specs = [q_spec, pl.BlockSpec((T, 2 * ATT_KV), lambda b, i: (b, ckv)), kvc_spec,
                    tq_tab, tq_tab, tq_tab, tk_tab, tk_tab, tk_tab, g_spec, g_spec]
        args = (u, u, u, tabs[0], tabs[1], tabs[2], tabs[0], tabs[1], tabs[2], qg, kg)
    else:
        in_specs = [q_spec, kvc_spec, g_spec, g_spec]
        args = (u, u, qg, kg)
    return pl.pallas_call(
        functools.partial(_att_kernel, T=T, C=C, tk=tk, use_rope=bool(T)),
        grid=(B, nq),
        in_specs=in_specs,
        out_specs=pl.BlockSpec((tq, ATT_Q), lambda b, i: (b * nq + i, 0)),
        out_shape=jax.ShapeDtypeStruct((B * Tq, ATT_Q), BF16),
        scratch_shapes=[pltpu.VMEM((ATT_KV_HEADS, S, LANES), BF16), pltpu.VMEM((ATT_KV_HEADS, ATT_V_ROWS, S), BF16),
                        pltpu.VMEM((ATT_HEADS, tq, LANES), BF16),
                        pltpu.VMEM((ATT_SCORE_LOOKAHEAD, tk, tq), F32)],
        compiler_params=pltpu.CompilerParams(dimension_semantics=("arbitrary", "arbitrary"),
                                             vmem_limit_bytes=VMEM_LIMIT_BYTES, flags=ATT_FLAGS),
        name="gqa_lat" if T else "gqa_ctx",
    )(*args)


def _merge_kernel(*refs, n_seq_tiles, fill_tail, **kw):
    xo_ref, h2_ref, lg_ref = refs[-3:]
    if not fill_tail:
        _merge_body(*refs, **kw)
        return
    i = pl.program_id(0)

    @pl.when(i < n_seq_tiles)
    def _():
        _merge_body(*refs, **kw)

    @pl.when(i >= n_seq_tiles)
    def _():
        for r in (xo_ref, h2_ref, lg_ref):
            r[...] = jnp.zeros_like(r)


def _merge_body(*refs, T, tm, tiles_per_batch, row_base):
    (of_ref, ob_ref, gr_ref, pu_ref, pp_ref, pn_ref, gt_ref, ya_ref, x_ref, g1_ref, sh2_ref, sc2_ref,
     gng_ref, band_ref, pw_ref, ps_ref, wb_ref, wo_ref, n2g_ref, wrh_ref, wrl_ref, br_ref) = refs[:22]
    xo_ref, h2_ref, lg_ref = refs[-3:]
    i = pl.program_id(0)
    it = i % tiles_per_batch
    b = i // tiles_per_batch if row_base is None else row_base

    gt = gt_ref[...].astype(F32)
    z = _sigmoid(gt[:, D_MODEL:2 * D_MODEL]) * _dot(ya_ref[...], wb_ref[1])

    prev = jnp.where(it > 0, pp_ref[...], jnp.zeros_like(pp_ref[...]))
    nxt = jnp.where(it < tiles_per_batch - 1, pn_ref[...], jnp.zeros_like(pn_ref[...]))
    ext = jnp.concatenate([prev, pu_ref[...], nxt], axis=0)
    sub = 128
    n_sub = tm // sub
    ext_sub = [ext[r * sub:r * sub + sub + 2 * POOL_HALO, :] for r in range(n_sub)]
    groups = [slice(gi * POOL_GROUP, (gi + 1) * POOL_GROUP) for gi in range(len(POOL_WINDOWS))]
    wsums = [[_dot(band_ref[gi], e[:, gs]) for gi, gs in enumerate(groups)] for e in ext_sub]
    yp_rows = []
    for r, e in enumerate(ext_sub):
        t = it * tm + r * sub + lax.broadcasted_iota(jnp.int32, (sub, 1), 0)
        cols = []
        for gi, win in enumerate(POOL_WINDOWS):
            cnt = (jnp.minimum(t + win // 2, T) - jnp.maximum(t - win // 2, 0)).astype(F32)
            d = wsums[r][gi] / cnt - e[POOL_HALO:POOL_HALO + sub, groups[gi]].astype(F32)
            cols.append(_dot(d.astype(BF16), pw_ref[gi]))
        yp_rows.append(jnp.concatenate(cols, axis=1))
    y_pool = (jnp.concatenate(yp_rows, axis=0) * ps_ref[...]).astype(BF16)
    z = z + _sigmoid(gt[:, 2 * D_MODEL:]) * _dot(y_pool, wb_ref[2])

    o = of_ref[...] + ob_ref[...]
    gr = gr_ref[...].astype(F32)
    parts = []
    for h in range(GLA_HEADS):
        hs = slice(h * GLA_DV, (h + 1) * GLA_DV)
        oh = o[:, hs]
        ms = jnp.mean(oh * oh, axis=-1, keepdims=True)
        parts.append(oh * lax.rsqrt(ms + NORM_EPS) * gng_ref[...])
    y_gla = (jnp.concatenate(parts, axis=1) * _silu(gr)).astype(BF16)
    z = z + _sigmoid(gt[:, :D_MODEL]) * _dot(y_gla, wb_ref[0])
    y = _dot(z.astype(BF16), wo_ref[...])
    xn = x_ref[...] + g1_ref[pl.ds(b, 1), :] * y
    xo_ref[...] = xn
    ms = jnp.mean(xn * xn, axis=-1, keepdims=True)
    h2 = xn * lax.rsqrt(ms + NORM_EPS) * n2g_ref[...]
    h2 = h2 * (1.0 + sc2_ref[pl.ds(b, 1), :]) + sh2_ref[pl.ds(b, 1), :]
    _store_token_rows(h2_ref, h2)
    hh, hl = _split_bf16(h2)
    lg_ref[...] = _dot(hh, wrh_ref[...]) + _dot(hh, wrl_ref[...]) + _dot(hl, wrh_ref[...]) + br_ref[...]


def _merge_call(o_f, o_b, u, y_att, xin, x_row_off, B, T, mods, l, mod_row, wts, n_out_rows, prev_outs):
    tm = min(TM_MERGE, T)
    tpb = T // tm
    D = D_MODEL
    R = mods.shape[2]
    ro = x_row_off // tm
    ro16 = x_row_off // POOL_HALO
    r16 = tm // POOL_HALO
    n16 = u.shape[0] // POOL_HALO
    cgr, cpu = OFF_GR // 512, OFF_PU // 512
    n_seq_tiles = B * tpb
    n_grid = n_seq_tiles if prev_outs is not None else (n_out_rows - x_row_off) // tm
    fill_tail = n_grid > n_seq_tiles
    ic = lambda i: jnp.minimum(i, n_seq_tiles - 1)
    full = lambda shp: pl.BlockSpec(shp, lambda i: (0,) * len(shp))
    mod_spec = lambda k: pl.BlockSpec((None, None, R, D), lambda i: (l, k, 0, 0))
    in_specs = [
        pl.BlockSpec((tm, GLA_V), lambda i: (ic(i), 0)),
        pl.BlockSpec((tm, GLA_V), lambda i: (ic(i), 0)),
        pl.BlockSpec((tm, 512), lambda i: (ro + ic(i), cgr)),
        pl.BlockSpec((tm, 512), lambda i: (ro + ic(i), cpu)),
        pl.BlockSpec((POOL_HALO, 512), lambda i: (jnp.maximum(ro16 + ic(i) * r16 - 1, 0), cpu)),
        pl.BlockSpec((POOL_HALO, 512), lambda i: (jnp.minimum(ro16 + (ic(i) + 1) * r16, n16 - 1), cpu)),
        pl.BlockSpec((tm, N_BRANCH * D), lambda i: (ro + ic(i), 0)),
        pl.BlockSpec((tm, ATT_Q), lambda i: (ic(i), 0)),
        pl.BlockSpec((tm, D), lambda i: (ro + ic(i), 0)),
        mod_spec(2), mod_spec(3), mod_spec(4),
        full((1, GLA_DV)), full((4, 128, 128 + 2 * POOL_HALO)), full((4, POOL_GROUP, POOL_GROUP)), full((1, POOL_WIDTH)),
        full((N_BRANCH, BRANCH_WIDTH, D)), full((D, D)), full((1, D)), full((D, ROUTER_W)), full((D, ROUTER_W)),
        full((1, ROUTER_W)),
    ]
    args = [o_f, o_b, u, u, u, u, u, y_att, xin, mods, mods, mods, *wts]
    out_shape = [jax.ShapeDtypeStruct((n_out_rows, D), F32), jax.ShapeDtypeStruct((n_out_rows * TOK_ROWS, LANES), F32),
                 jax.ShapeDtypeStruct((n_out_rows, ROUTER_W), F32)]
    out_specs = [pl.BlockSpec((tm, D), lambda i: (ro + i, 0)), pl.BlockSpec((tm * TOK_ROWS, LANES), lambda i: (ro + i, 0)),
                 pl.BlockSpec((tm, ROUTER_W), lambda i: (ro + i, 0))]
    aliases = {}
    if prev_outs is not None:
        n_in = len(args)
        in_specs += [pl.BlockSpec(memory_space=pl.ANY)] * 3
        args += list(prev_outs)
        aliases = {n_in: 0, n_in + 1: 1, n_in + 2: 2}
    kern = functools.partial(_merge_kernel, n_seq_tiles=n_seq_tiles, fill_tail=fill_tail,
                             T=T, tm=tm, tiles_per_batch=tpb, row_base=mod_row)
    return pl.pallas_call(
        kern,
        grid=(n_grid,),
        in_specs=in_specs,
        out_specs=out_specs,
        out_shape=out_shape,
        input_output_aliases=aliases,
        compiler_params=_cparams(("arbitrary",)),
        name="merge",
    )(*args)


def _router_kernel(lg_ref, out_ref, cnt_ref, carry, pstart, *, tm, bm):
    ph = pl.program_id(0)
    i = pl.program_id(1)
    lane = lax.broadcasted_iota(jnp.int32, (1, LANES), 1)
    neg = -jnp.inf
    x = lg_ref[...]

    @pl.when((ph == 0) & (i == 0))
    def _():
        carry[...] = jnp.zeros_like(carry)

    gl = jnp.where(lane < N_GROUPS, x, neg)
    gmax = jnp.max(gl, axis=-1, keepdims=True)
    gsum = jnp.sum(jnp.exp(gl - gmax), axis=-1, keepdims=True)
    grp = jnp.min(jnp.where(gl == gmax, lane, LANES), axis=-1, keepdims=True)
    e_lane = lane - N_GROUPS
    lane_grp = sum((e_lane >= EXP_PER_GROUP * k).astype(jnp.int32) for k in range(1, N_GROUPS))
    lane_grp = jnp.where(e_lane < 0, -1, jnp.where(e_lane < N_EXPERTS, lane_grp, -1))
    in_grp = lane_grp == grp
    el = jnp.where(in_grp, x, neg)
    emax = jnp.max(el, axis=-1, keepdims=True)
    esum = jnp.sum(jnp.exp(el - emax), axis=-1, keepdims=True)
    i1 = jnp.min(jnp.where(el == emax, lane, LANES), axis=-1, keepdims=True)
    el2 = jnp.where(lane == i1, neg, el)
    m2 = jnp.max(el2, axis=-1, keepdims=True)
    i2 = jnp.min(jnp.where(el2 == m2, lane, LANES), axis=-1, keepdims=True)
    p1 = 1.0 / esum
    p2 = jnp.exp(m2 - emax) / esum
    pg = 1.0 / gsum
    w1 = pg * p1 / (p1 + p2)
    w2 = pg * p2 / (p1 + p2)
    oh1 = lane == i1
    oh2 = lane == i2
    oh = jnp.where(oh1, 1.0, jnp.where(oh2, 1.0, 0.0))

    @pl.when(ph == 0)
    def _():
        carry[...] += jnp.sum(oh, axis=0, keepdims=True)

    @pl.when((ph == 1) & (i == 0))
    def _():
        cnt = carry[...]
        cnt_ref[...] = jnp.broadcast_to(cnt, cnt_ref.shape)
        nb = jnp.floor((cnt + (bm - 1)) * (1.0 / bm))
        hi = jnp.floor(nb * (1.0 / 16.0))
        lo = nb - 16.0 * hi
        r = lax.broadcasted_iota(jnp.int32, (LANES, LANES), 0)
        c = lax.broadcasted_iota(jnp.int32, (LANES, LANES), 1)
        upper = jnp.where(r < c, 1.0, 0.0).astype(BF16)
        hi8 = jnp.broadcast_to(hi, (8, LANES)).astype(BF16)
        lo8 = jnp.broadcast_to(lo, (8, LANES)).astype(BF16)
        pre = 16.0 * _dot(hi8, upper) + _dot(lo8, upper)
        pstart[...] = pre[0:1, :] * bm
        carry[...] = jnp.zeros_like(carry)

    @pl.when(ph == 1)
    def _():
        r = lax.broadcasted_iota(jnp.int32, (tm, tm), 0)
        c = lax.broadcasted_iota(jnp.int32, (tm, tm), 1)
        lower = jnp.where(c < r, 1.0, 0.0).astype(BF16)
        before = _dot(lower, oh.astype(BF16))
        slot = pstart[...] + carry[...] + before
        d1 = jnp.sum(jnp.where(oh1, slot, 0.0), axis=-1, keepdims=True)
        d2 = jnp.sum(jnp.where(oh2, slot, 0.0), axis=-1, keepdims=True)
        carry[...] += jnp.sum(oh, axis=0, keepdims=True)
        out_ref[...] = jnp.where(lane == 0, d1, jnp.where(lane == 1, d2, jnp.where(lane == 2, w1, jnp.where(lane == 3, w2, 0.0))))


def _router_call(logits, bm):
    N = logits.shape[0]
    tm = TM_ROUTE
    return pl.pallas_call(
        functools.partial(_router_kernel, tm=tm, bm=bm),
        grid=(2, N // tm),
        in_specs=[pl.BlockSpec((tm, ROUTER_W), lambda p, i: (i, 0))],
        out_specs=[pl.BlockSpec((tm, LANES), lambda p, i: (i * p, 0)), pl.BlockSpec((8, LANES), lambda p, i: (0, 0))],
        out_shape=[jax.ShapeDtypeStruct((N, LANES), F32), jax.ShapeDtypeStruct((8, LANES), F32)],
        scratch_shapes=[pltpu.VMEM((1, LANES), F32), pltpu.VMEM((1, LANES), F32)],
        compiler_params=_cparams(("arbitrary", "arbitrary")),
        name="router",
    )(logits)


def _tok_rows(t):
    return pl.ds(pl.multiple_of(t * TOK_ROWS, TOK_ROWS), TOK_ROWS)


def _dispatch_kernel(idx_hbm, h2_ref, xs_in, xs_hbm, idx_s, isem, dsem, *, tm, n_tiles):
    del xs_in
    i = pl.program_id(0)
    s = i % 2

    def idx_copy(j):
        return pltpu.make_async_copy(idx_hbm.at[j], idx_s.at[j % 2], isem.at[j % 2])

    def row_copy(r, slot):
        return pltpu.make_async_copy(h2_ref.at[_tok_rows(r), :], xs_hbm.at[_tok_rows(slot), :], dsem)

    @pl.when(i == 0)
    def _():
        idx_copy(0).start()

    @pl.when(i + 1 < n_tiles)
    def _():
        idx_copy(i + 1).start()

    idx_copy(i).wait()

    def issue(r0, c):
        for u in range(DMA_UNROLL):
            r = r0 * DMA_UNROLL + u
            for k in range(TOP_K):
                row_copy(r, idx_s[s, 0, TOP_K * r + k]).start()
        return c

    lax.fori_loop(0, tm // DMA_UNROLL, issue, 0)

    def drain(r0, c):
        for _ in range(TOP_K * DMA_UNROLL):
            row_copy(0, 0).wait()
        return c

    lax.fori_loop(0, tm // DMA_UNROLL, drain, 0)


def _dispatch_call(idx, h2, n_slots):
    n_tiles = idx.shape[0]
    tm = TM_DISP
    xs0 = jnp.zeros((n_slots * TOK_ROWS, LANES), F32)
    return pl.pallas_call(
        functools.partial(_dispatch_kernel, tm=tm, n_tiles=n_tiles),
        grid=(n_tiles,),
        in_specs=[pl.BlockSpec(memory_space=pl.ANY),
                  pl.BlockSpec((tm * TOK_ROWS, LANES), lambda i: (i, 0)),
                  pl.BlockSpec(memory_space=pl.ANY)],
        out_specs=pl.BlockSpec(memory_space=pl.ANY),
        out_shape=jax.ShapeDtypeStruct((n_slots * TOK_ROWS, LANES), F32),
        input_output_aliases={2: 0},
        scratch_shapes=[pltpu.SMEM((2, 1, TOP_K * tm), jnp.int32), pltpu.SemaphoreType.DMA((2,)),
                        pltpu.SemaphoreType.DMA(())],
        compiler_params=pltpu.CompilerParams(dimension_semantics=("arbitrary",), vmem_limit_bytes=VMEM_LIMIT_BYTES,
                                             has_side_effects=True),
        name="dispatch",
    )(idx, h2, xs0)


def _expert_kernel(be_ref, nu_ref, xs_ref, wg_ref, wu_ref, wd_ref, ys_ref, wgb, wub, wdb, *, bm):
    i = pl.program_id(0)

    @pl.when((i == 0) | (be_ref[i] != be_ref[jnp.maximum(i - 1, 0)]))
    def _():
        wgb[...] = wg_ref[...].astype(BF16)
        wub[...] = wu_ref[...].astype(BF16)
        wdb[...] = wd_ref[...].astype(BF16)

    @pl.when(i < nu_ref[0])
    def _():
        xb = _load_token_rows(xs_ref, bm).astype(BF16)
        g = _dot(xb, wgb[...])
        up = _dot(xb, wub[...])
        hmid = (_silu(g) * up).astype(BF16)
        _store_token_rows(ys_ref, _dot(hmid, wdb[...]))

    @pl.when(i >= nu_ref[0])
    def _():
        ys_ref[...] = jnp.zeros_like(ys_ref)


def _expert_call(blk_expert, nused, xs, wg, wu, wd, l):
    bm = BM_MOE
    D = D_MODEL
    nblk = blk_expert.shape[0]
    w_in_spec = pl.BlockSpec((None, None, D, D_EXPERT), lambda i, be, nu: (l, be[i], 0, 0))
    grid_spec = pltpu.PrefetchScalarGridSpec(
        num_scalar_prefetch=2,
        grid=(nblk,),
        in_specs=[
            pl.BlockSpec((bm * TOK_ROWS, LANES), lambda i, be, nu: (jnp.minimum(i, nu[0] - 1), 0)),
            w_in_spec, w_in_spec,
            pl.BlockSpec((None, None, D_EXPERT, D), lambda i, be, nu: (l, be[i], 0, 0)),
        ],
        out_specs=pl.BlockSpec((bm * TOK_ROWS, LANES), lambda i, be, nu: (i, 0)),
        scratch_shapes=[pltpu.VMEM((D, D_EXPERT), BF16), pltpu.VMEM((D, D_EXPERT), BF16),
                        pltpu.VMEM((D_EXPERT, D), BF16)],
    )
    return pl.pallas_call(
        functools.partial(_expert_kernel, bm=bm),
        grid_spec=grid_spec,
        out_shape=jax.ShapeDtypeStruct(xs.shape, F32),
        compiler_params=_cparams(("arbitrary",)),
        name="experts",
    )(blk_expert, nused, xs, wg, wu, wd)


def _combine_kernel(idx_hbm, ys_hbm, x_ref, r_ref, g2_ref, fg_ref, o_ref, idx_s, isem, ybuf, gsem, *,
                    tm, n_tiles, n_lat_tiles, tiles_per_batch, ctx_row, final):
    i = pl.program_id(0)

    def idx_copy(j):
        return pltpu.make_async_copy(idx_hbm.at[j], idx_s.at[j % 3], isem.at[j % 3])

    def row_copy(j, r, k, slot):
        return pltpu.make_async_copy(ys_hbm.at[_tok_rows(slot), :], ybuf.at[j % 2, _tok_rows(k * tm + r), :],
                                     gsem.at[j % 2])

    def start_gather(j):
        def issue(r0, c):
            for u in range(DMA_UNROLL):
                r = r0 * DMA_UNROLL + u
                for k in range(TOP_K):
                    row_copy(j, r, k, idx_s[j % 3, 0, TOP_K * r + k]).start()
            return c
        lax.fori_loop(0, tm // DMA_UNROLL, issue, 0)

    def wait_gather(j):
        def drain(r0, c):
            for _ in range(TOP_K * DMA_UNROLL):
                row_copy(j, 0, 0, 0).wait()
            return c
        lax.fori_loop(0, tm // DMA_UNROLL, drain, 0)

    @pl.when(i == 0)
    def _():
        idx_copy(0).start()
        idx_copy(0).wait()
        start_gather(0)
        if n_tiles > 1:
            idx_copy(1).start()

    @pl.when(i + 1 < n_tiles)
    def _():
        idx_copy(i + 1).wait()
        start_gather(i + 1)

    @pl.when(i + 2 < n_tiles)
    def _():
        idx_copy(i + 2).start()

    wait_gather(i)
    yb = ybuf.at[i % 2]
    y0 = _load_token_rows(yb, tm, 0)
    y1 = _load_token_rows(yb, tm, tm * TOK_ROWS)
    m = r_ref[:, 2:3] * y0 + r_ref[:, 3:4] * y1
    b = jnp.where(i < n_lat_tiles, i // tiles_per_batch, ctx_row)
    xn = x_ref[...] + g2_ref[pl.ds(b, 1), :] * m
    if final:
        ms = jnp.mean(xn * xn, axis=-1, keepdims=True)
        xn = xn * lax.rsqrt(ms + NORM_EPS) * fg_ref[...]
    o_ref[...] = xn


def _combine_call(idx, ys, x_mid, route, mods, l, final_g, n_lat_rows, rows_per_batch, ctx_row, final):
    tm = TM_COMB
    D = D_MODEL
    R = mods.shape[2]
    n_rows = x_mid.shape[0]
    n_tiles = n_rows // tm
    kern = functools.partial(_combine_kernel, tm=tm, n_tiles=n_tiles, n_lat_tiles=n_lat_rows // tm,
                             tiles_per_batch=rows_per_batch // tm, ctx_row=ctx_row, final=final)
    return pl.pallas_call(
        kern,
        grid=(n_tiles,),
        in_specs=[
            pl.BlockSpec(memory_space=pl.ANY),
            pl.BlockSpec(memory_space=pl.ANY),
            pl.BlockSpec((tm, D), lambda i: (i, 0)),
            pl.BlockSpec((tm, LANES), lambda i: (i, 0)),
            pl.BlockSpec((None, None, R, D), lambda i: (l, 5, 0, 0)),
            pl.BlockSpec((1, D), lambda i: (0, 0)),
        ],
        out_specs=pl.BlockSpec((tm, D), lambda i: (i, 0)),
        out_shape=jax.ShapeDtypeStruct((n_rows, D), F32),
        scratch_shapes=[pltpu.SMEM((3, 1, TOP_K * tm), jnp.int32), pltpu.SemaphoreType.DMA((3,)),
                        pltpu.VMEM((2, TOP_K * tm * TOK_ROWS, LANES), F32), pltpu.SemaphoreType.DMA((2,))],
        compiler_params=_cparams(("arbitrary",)),
        name="combine",
    )(idx, ys, x_mid, route, mods, final_g.reshape(1, D))


def _permute_w_in(w_in):
    gq, gk, gv, gr, glf, glb, aq, ak, av, pu, gt = jnp.split(w_in, np.cumsum(IN_SIZES)[:-1].tolist(), axis=-1)
    pad = jnp.zeros(w_in.shape[:-1] + (U_WIDTH - sum(IN_SIZES),), w_in.dtype)
    return jnp.concatenate([gt, gq, gk, gv, gr, aq, pu, ak, av, glf, glb, pad], axis=-1).astype(BF16)


def _rope_tables(T):
    rows = T // GRID_W
    row = np.repeat(np.arange(rows), GRID_W).astype(np.float32)
    col = np.tile(np.arange(GRID_W), rows).astype(np.float32)
    inv = jnp.asarray(ROPE_THETA, F32) ** (-jnp.arange(0, ROPE_AXIS_DIM, 2, dtype=F32) / ROPE_AXIS_DIM)
    ang_r = jnp.asarray(row)[:, None] * inv
    ang_c = jnp.asarray(col)[:, None] * inv
    zero = jnp.zeros_like(ang_r)
    cos = jnp.concatenate([jnp.cos(ang_r)] * 2 + [jnp.cos(ang_c)] * 2, axis=1)
    s1 = jnp.concatenate([-jnp.sin(ang_r), zero, -jnp.sin(ang_c), zero], axis=1)
    s2 = jnp.concatenate([zero, jnp.sin(ang_r), zero, jnp.sin(ang_c)], axis=1)
    return tuple(jnp.concatenate([t, t], axis=1) for t in (cos, s1, s2))


def _pool_bands():
    i = np.arange(128)[:, None]
    j = np.arange(128 + 2 * POOL_HALO)[None, :]
    bands = [((j >= i + POOL_HALO - w // 2) & (j < i + POOL_HALO + w // 2)).astype(np.float32) for w in POOL_WINDOWS]
    return jnp.asarray(np.stack(bands), BF16)


def _block_table(counts, n_tok, bm):
    cnt = counts[0, N_GROUPS:N_GROUPS + N_EXPERTS].astype(jnp.int32)
    pad_end = jnp.cumsum((cnt + bm - 1) // bm * bm)
    nblk = -(-(n_tok * TOP_K + N_EXPERTS * (bm - 1)) // bm)
    blk_start = jnp.arange(nblk, dtype=jnp.int32) * bm
    blk_expert = jnp.minimum(jnp.sum((pad_end[None, :] <= blk_start[:, None]).astype(jnp.int32), axis=1), N_EXPERTS - 1)
    nused = (pad_end[-1] // bm).astype(jnp.int32).reshape(1)
    return blk_expert.astype(jnp.int32), nused, nblk


def _slot_tiles(route, tm):
    n = route.shape[0]
    return route[:, :TOP_K].astype(jnp.int32).reshape(n // tm, 1, TOP_K * tm)


def kernel(x, c, ctx, c_ctx, w_mod, b_mod, norm1_g, norm2_g, w_in, gla_a_up_f, gla_a_bias_f, gla_a_up_b, gla_a_bias_b, gla_norm_g, att_qn_g, att_kn_g, pool_w, pool_scale, w_branch, w_out, moe_w_group, moe_b_group, moe_w_expert, moe_b_expert, moe_w_gate, moe_w_up, moe_w_down, final_g):
    B, T, D = x.shape
    C = ctx.shape[1]
    L = w_mod.shape[0]
    n_lat, n_ctx = B * T, B * C
    MOD_ROWS = 16
    assert D == D_MODEL and B < MOD_ROWS and T % TQ_ATT == 0 and C % GLA_CHUNK == 0

    s_in = jnp.concatenate([c, c_ctx[None], jnp.zeros((MOD_ROWS - B - 1, D), F32)], axis=0)
    mods = _mod_call(s_in, w_mod, b_mod)
    w_perm = _permute_w_in(w_in)
    tabs = _rope_tables(T)
    bands = _pool_bands()
    zero_state = jnp.zeros((B, GLA_V, GLA_QK), F32)

    xall = jnp.concatenate([x.reshape(n_lat, D), ctx.reshape(n_ctx, D)], axis=0)
    out = None
    for l in range(L):
        want_ctx = l < L - 1
        n_rows = xall.shape[0]
        u = _inproj_call(xall, mods, l, norm1_g[l], w_perm[l], n_lat, T, B)

        up = jnp.zeros((128, 2 * GLA_QK), F32)
        up = up.at[:GLA_RANK, :GLA_QK].set(gla_a_up_f[l]).at[GLA_RANK:2 * GLA_RANK, GLA_QK:].set(gla_a_up_b[l]).astype(BF16)
        bias = jnp.concatenate([gla_a_bias_f[l], gla_a_bias_b[l]]).reshape(1, 2 * GLA_QK)
        ofc, obc, sfc, sbc = _gla_call(u, n_lat, B, C, zero_state, zero_state, up, bias)
        of, ob, _, _ = _gla_call(u, 0, B, T, sfc, sbc, up, bias)

        qg = jnp.tile(att_qn_g[l], 2).reshape(1, LANES)
        kg = jnp.tile(att_kn_g[l], 2).reshape(1, LANES)
        ya = _att_call(u, 0, B, T, T, C, n_lat, tabs, qg, kg)

        wr = jnp.zeros((D, ROUTER_W), F32).at[:, :N_GROUPS].set(moe_w_group[l]).at[:, N_GROUPS:N_GROUPS + N_EXPERTS].set(moe_w_expert[l])
        wrh, wrl = _split_bf16(wr)
        br = jnp.zeros((1, ROUTER_W), F32).at[0, :N_GROUPS].set(moe_b_group[l]).at[0, N_GROUPS:N_GROUPS + N_EXPERTS].set(moe_b_expert[l])
        wts = (gla_norm_g[l].reshape(1, GLA_DV), bands, pool_w[l].astype(BF16), pool_scale[l].reshape(1, POOL_WIDTH),
               w_branch[l].astype(BF16), w_out[l].astype(BF16), norm2_g[l].reshape(1, D), wrh, wrl, br)
        n_tok = n_rows if want_ctx else n_lat
        outs = _merge_call(of, ob, u, ya, xall, 0, B, T, mods, l, None, wts, n_tok, None)
        if want_ctx:
            yac = _att_call(u, n_lat, B, C, 0, C, n_lat, None, qg, kg)
            outs = _merge_call(ofc, obc, u, yac, xall, n_lat, B, C, mods, l, B, wts, n_tok, outs)
        x_mid, h2, logits = outs

        route, counts = _router_call(logits, BM_MOE)
        blk_expert, nused, nblk = _block_table(counts, n_tok, BM_MOE)
        xs = _dispatch_call(_slot_tiles(route, TM_DISP), h2, nblk * BM_MOE)
        ys = _expert_call(blk_expert, nused, xs, moe_w_gate, moe_w_up, moe_w_down, l)
        xall = _combine_call(_slot_tiles(route, TM_COMB), ys, x_mid, route, mods, l, final_g, n_lat, T, B,
                             final=not want_ctx)
    return xall[:n_lat].reshape(B, T, D)
```

```python
import functools

import numpy as np
import jax
import jax.numpy as jnp
from jax import lax
from jax.experimental import pallas as pl
from jax.experimental.pallas import tpu as pltpu

F32 = jnp.float32
BF16 = jnp.bfloat16

VMEM_LIMIT_BYTES = 56 * 1024 * 1024
LANES = 128

D_MODEL = 1024
GRID_W = 64
NORM_EPS = 1e-6
GLA_HEADS, GLA_DK, GLA_DV, GLA_RANK, GLA_TAU, GLA_CHUNK = 4, 64, 128, 16, 16.0, 64
GLA_QK, GLA_V = GLA_HEADS * GLA_DK, GLA_HEADS * GLA_DV
ATT_HEADS, ATT_KV_HEADS, ATT_DH = 8, 2, 64
ROPE_THETA, ROPE_AXIS_DIM = 10000.0, 32
ATT_Q, ATT_KV = ATT_HEADS * ATT_DH, ATT_KV_HEADS * ATT_DH
POOL_WINDOWS, POOL_GROUP = (2, 4, 8, 16), 128
POOL_WIDTH = POOL_GROUP * len(POOL_WINDOWS)
POOL_HALO = 16
N_BRANCH, BRANCH_WIDTH = 3, 512
N_GROUPS, EXP_PER_GROUP, TOP_K, D_EXPERT = 4, 8, 2, 512
N_EXPERTS = N_GROUPS * EXP_PER_GROUP
IN_SIZES = (GLA_QK, GLA_QK, GLA_V, GLA_V, GLA_RANK, GLA_RANK, ATT_Q, ATT_KV, ATT_KV, POOL_WIDTH, N_BRANCH * D_MODEL)

OFF_GT, OFF_GQK, OFF_GV, OFF_GR, OFF_AQ, OFF_PU, OFF_AKV, OFF_GL = 0, 3072, 3584, 4096, 4608, 5120, 5632, 5888
U_WIDTH = 6144
U_CHUNK = 512

TM_IN = 512
TB_GLA = 256
GLA_CHUNKS_PER_BODY = 4
TQ_ATT = 512
TK_ATT = 512
ATT_SUB_ROWS = 64
LOG2_E = 1.4426950408889634
ATT_FLAGS = {}
ATT_V_ROWS = 80
ATT_SCORE_LOOKAHEAD = 2
TM_MERGE = 512
BM_MOE = 256
TM_ROUTE = 512
TM_DISP = 256
TM_COMB = 256
ROUTER_W = 128
TOK_ROWS = D_MODEL // LANES
DMA_UNROLL = 8


def _cparams(sem):
    return pltpu.CompilerParams(dimension_semantics=sem, vmem_limit_bytes=VMEM_LIMIT_BYTES)


def _split_bf16(a):
    hi = a.astype(BF16)
    lo = (a - hi.astype(F32)).astype(BF16)
    return hi, lo


def _dot(a, b):
    return jnp.dot(a, b, preferred_element_type=F32)


def _dot_nt(a, b):
    return lax.dot_general(a, b, (((1,), (1,)), ((), ())), preferred_element_type=F32)


def _dot_tn(a, b):
    return lax.dot_general(a, b, (((0,), (0,)), ((), ())), preferred_element_type=F32)


def _dot3(a, b):
    ah, al = _split_bf16(a)
    bh, bl = _split_bf16(b)
    return _dot(ah, bh) + _dot(ah, bl) + _dot(al, bh)


def _load_token_rows(ref, n_tok, row0=0):
    return jnp.concatenate([ref[pl.ds(row0 + s, n_tok, stride=TOK_ROWS), :] for s in range(TOK_ROWS)], axis=1)


def _store_token_rows(ref, val):
    n_tok = val.shape[0]
    for s in range(TOK_ROWS):
        ref[pl.ds(s, n_tok, stride=TOK_ROWS), :] = val[:, s * LANES:(s + 1) * LANES]


def _sigmoid(x):
    return 1.0 / (1.0 + jnp.exp(-x))


def _silu(x):
    return x * _sigmoid(x)


def _mod_kernel(s_ref, w_ref, b_ref, o_ref):
    s = _silu(s_ref[...])
    o_ref[...] = _dot3(s, w_ref[...]) + b_ref[...]


def _mod_call(s_in, w_mod, b_mod):
    L, D, _ = w_mod.shape
    R = s_in.shape[0]
    return pl.pallas_call(
        _mod_kernel,
        grid=(L, 6),
        in_specs=[
            pl.BlockSpec((R, D), lambda l, j: (0, 0)),
            pl.BlockSpec((None, D, D), lambda l, j: (l, 0, j)),
            pl.BlockSpec((None, None, 1, D), lambda l, j: (l, j, 0, 0)),
        ],
        out_specs=pl.BlockSpec((None, None, R, D), lambda l, j: (l, j, 0, 0)),
        out_shape=jax.ShapeDtypeStruct((L, 6, R, D), F32),
        compiler_params=_cparams(("arbitrary", "arbitrary")),
        name="mod_table",
    )(s_in, w_mod, b_mod.reshape(L, 6, 1, D))


def _inproj_kernel(x_ref, sh_ref, sc_ref, g_ref, w_ref, o_ref, *, n_lat_tiles, tiles_per_batch, ctx_row):
    i = pl.program_id(0)
    b = jnp.where(i < n_lat_tiles, i // tiles_per_batch, ctx_row)
    x = x_ref[...]
    ms = jnp.mean(x * x, axis=-1, keepdims=True)
    h = x * lax.rsqrt(ms + NORM_EPS) * g_ref[...]
    h = h * (1.0 + sc_ref[pl.ds(b, 1), :]) + sh_ref[pl.ds(b, 1), :]
    hb = h.astype(BF16)
    for c in range(U_WIDTH // U_CHUNK):
        cs = slice(c * U_CHUNK, (c + 1) * U_CHUNK)
        o_ref[:, cs] = _dot(hb, w_ref[:, cs]).astype(BF16)


def _inproj_call(x, mods, l, norm_g, w_perm, n_lat_rows, rows_per_batch, ctx_row):
    N, D = x.shape
    R = mods.shape[2]
    tm = TM_IN
    kern = functools.partial(_inproj_kernel, n_lat_tiles=n_lat_rows // tm,
                             tiles_per_batch=rows_per_batch // tm, ctx_row=ctx_row)
    return pl.pallas_call(
        kern,
        grid=(N // tm,),
        in_specs=[
            pl.BlockSpec((tm, D), lambda i: (i, 0)),
            pl.BlockSpec((None, None, R, D), lambda i: (l, 0, 0, 0)),
            pl.BlockSpec((None, None, R, D), lambda i: (l, 1, 0, 0)),
            pl.BlockSpec((1, D), lambda i: (0, 0)),
            pl.BlockSpec((D, U_WIDTH), lambda i: (0, 0)),
        ],
        out_specs=pl.BlockSpec((tm, U_WIDTH), lambda i: (i, 0)),
        out_shape=jax.ShapeDtypeStruct((N, U_WIDTH), BF16),
        compiler_params=_cparams(("arbitrary",)),
        name="inproj",
    )(x, mods, mods, norm_g.reshape(1, D), w_perm)


def _gla_chunks(work):
    Lc = GLA_CHUNK
    r = lax.broadcasted_iota(jnp.int32, (Lc, Lc), 0)
    c = lax.broadcasted_iota(jnp.int32, (Lc, Lc), 1)
    lane_head = lax.broadcasted_iota(jnp.int32, (Lc, GLA_QK), 1) // GLA_DK
    vlane_head = lax.broadcasted_iota(jnp.int32, (Lc, GLA_V), 1) // GLA_DV
    t_i = lax.broadcasted_iota(jnp.int32, (Lc, GLA_HEADS * Lc), 0)
    s_i = lax.broadcasted_iota(jnp.int32, (Lc, GLA_HEADS * Lc), 1) % Lc
    row_head = lax.broadcasted_iota(jnp.int32, (GLA_V, GLA_QK), 0) // GLA_DV
    col_head = lax.broadcasted_iota(jnp.int32, (GLA_V, GLA_QK), 1) // GLA_DK

    zs = [_dot(gl, up) + bias for (_, _, gl, up, bias, _, _) in work]
    bs = []
    for z, (_, _, _, _, _, _, forward) in zip(zs, work):
        la = (jnp.minimum(z, 0.0) - jnp.log(1.0 + jnp.exp(-jnp.abs(z)))) * (1.0 / GLA_TAU)
        tri = jnp.where((r >= c) if forward else (r <= c), 1.0, 0.0).astype(BF16)
        la_hi, la_lo = _split_bf16(la)
        bs.append(_dot(tri, la_hi) + _dot(tri, la_lo))
    mids = []
    for b, (qk, v, _, _, _, _, forward) in zip(bs, work):
        q = qk[:, :GLA_QK].astype(F32) * (GLA_DK ** -0.5)
        k = qk[:, GLA_QK:].astype(F32)
        i_last, i_ref = (Lc - 1, Lc // 2) if forward else (0, Lc - 1 - Lc // 2)
        b_last = b[i_last:i_last + 1, :]
        b_ref = b[i_ref:i_ref + 1, :]
        qd = (q * jnp.exp(b - b_ref)).astype(BF16)
        kd = (k * jnp.exp(b_ref - b)).astype(BF16)
        qe = (q * jnp.exp(b)).astype(BF16)
        kl = (k * jnp.exp(b_last - b)).astype(BF16)
        a = jnp.exp(b_last)
        kd_blk = jnp.concatenate([jnp.where(lane_head == h, kd, jnp.zeros_like(kd)) for h in range(GLA_HEADS)], axis=0)
        v_blk = jnp.concatenate([jnp.where(vlane_head == h, v, jnp.zeros_like(v)) for h in range(GLA_HEADS)], axis=0)
        mids.append((qd, kd_blk, qe, kl, a, v_blk))
    scs = [_dot_nt(qd, kd_blk) for (qd, kd_blk, _, _, _, _) in mids]
    dss = [_dot_tn(v, kl) for (_, v, _, _, _, _, _), (_, _, _, kl, _, _) in zip(work, mids)]
    outs = []
    for sc, ds, (_, _, qe, _, a, v_blk), (_, _, _, _, _, st_ref, forward) in zip(scs, dss, mids, work):
        sc = jnp.where((s_i <= t_i) if forward else (s_i >= t_i), sc, 0.0).astype(BF16)
        st = st_ref[...]
        outs.append(_dot(sc, v_blk) + _dot_nt(qe, st.astype(BF16)))
        st_ref[...] = a * st + jnp.where(row_head == col_head, ds, 0.0)
    return outs


def _gla_kernel(qkf_ref, vf_ref, glf_ref, qkb_ref, vb_ref, glb_ref, up_ref, bias_ref, s0f_ref, s0b_ref,
                of_ref, ob_ref, sf_ref, sb_ref, stf, stb, *, nsub):
    n = pl.program_id(1)

    @pl.when(n == 0)
    def _():
        stf[...] = s0f_ref[...]
        stb[...] = s0b_ref[...]

    def body(j, carry):
        work, rows = [], []
        for u in range(GLA_CHUNKS_PER_BODY):
            jj = j * GLA_CHUNKS_PER_BODY + u
            rf = pl.ds(pl.multiple_of(jj * GLA_CHUNK, GLA_CHUNK), GLA_CHUNK)
            rb = pl.ds(pl.multiple_of((nsub - 1 - jj) * GLA_CHUNK, GLA_CHUNK), GLA_CHUNK)
            work += [
                (qkf_ref[rf, :], vf_ref[rf, :], glf_ref[rf, :], up_ref[:, :GLA_QK], bias_ref[:, :GLA_QK], stf, True),
                (qkb_ref[rb, :], vb_ref[rb, :], glb_ref[rb, :], up_ref[:, GLA_QK:], bias_ref[:, GLA_QK:], stb, False)]
            rows += [(of_ref, rf), (ob_ref, rb)]
        for (ref, rr), o in zip(rows, _gla_chunks(work)):
            ref[rr, :] = o
        return carry

    lax.fori_loop(0, nsub // GLA_CHUNKS_PER_BODY, body, 0)

    @pl.when(n == pl.num_programs(1) - 1)
    def _():
        sf_ref[...] = stf[...]
        sb_ref[...] = stb[...]


def _gla_call(u, row_off, B, T, s0f, s0b, up, bias):
    tb = min(TB_GLA, T)
    nb = T // tb
    off = row_off // tb
    fwd = lambda b, n: off + b * nb + n
    bwd = lambda b, n: off + b * nb + (nb - 1 - n)
    cqk, cv, cgl = OFF_GQK // 512, OFF_GV // 512, OFF_GL // 128
    st_spec = pl.BlockSpec((None, GLA_V, GLA_QK), lambda b, n: (b, 0, 0))
    st_shape = jax.ShapeDtypeStruct((B, GLA_V, GLA_QK), F32)
    o_shape = jax.ShapeDtypeStruct((B * T, GLA_V), F32)
    return pl.pallas_call(
        functools.partial(_gla_kernel, nsub=tb // GLA_CHUNK),
        grid=(B, nb),
        in_specs=[
            pl.BlockSpec((tb, 512), lambda b, n: (fwd(b, n), cqk)),
            pl.BlockSpec((tb, 512), lambda b, n: (fwd(b, n), cv)),
            pl.BlockSpec((tb, 128), lambda b, n: (fwd(b, n), cgl)),
            pl.BlockSpec((tb, 512), lambda b, n: (bwd(b, n), cqk)),
            pl.BlockSpec((tb, 512), lambda b, n: (bwd(b, n), cv)),
            pl.BlockSpec((tb, 128), lambda b, n: (bwd(b, n), cgl)),
            pl.BlockSpec((128, 2 * GLA_QK), lambda b, n: (0, 0)),
            pl.BlockSpec((1, 2 * GLA_QK), lambda b, n: (0, 0)),
            st_spec, st_spec,
        ],
        out_specs=[
            pl.BlockSpec((tb, GLA_V), lambda b, n: (b * nb + n, 0)),
            pl.BlockSpec((tb, GLA_V), lambda b, n: (b * nb + (nb - 1 - n), 0)),
            st_spec, st_spec,
        ],
        out_shape=[o_shape, o_shape, st_shape, st_shape],
        scratch_shapes=[pltpu.VMEM((GLA_V, GLA_QK), F32), pltpu.VMEM((GLA_V, GLA_QK), F32)],
        compiler_params=_cparams(("arbitrary", "arbitrary")),
        name="gla_scan",
    )(u, u, u, u, u, u, up, bias, s0f, s0b)


def _group_ms64(x):
    i = lax.broadcasted_iota(jnp.int32, (LANES, LANES), 0) // ATT_DH
    j = lax.broadcasted_iota(jnp.int32, (LANES, LANES), 1) // ATT_DH
    bd = jnp.where(i == j, 1.0, 0.0).astype(BF16)
    hi, lo = _split_bf16(x * x)
    return (_dot(hi, bd) + _dot(lo, bd)) * (1.0 / ATT_DH)


def _rope128(x, cos, s1, s2):
    return x * cos + pltpu.roll(x, LANES - ROPE_AXIS_DIM // 2, axis=1) * s1 + pltpu.roll(x, ROPE_AXIS_DIM // 2, axis=1) * s2


def _att_kernel(*refs, T, C, tk, use_rope):
    kx, vx, qh = refs[-3:]
    if T:
        (q_ref, kvl_ref, kvc_ref, cq_ref, s1q_ref, s2q_ref, ck_ref, s1k_ref, s2k_ref, qg_ref, kg_ref,
         o_ref) = refs[:-3]
    else:
        q_ref, kvc_ref, qg_ref, kg_ref, o_ref = refs[:-3]
    i = pl.program_id(1)
    lo_half = lax.broadcasted_iota(jnp.int32, (1, LANES), 1) < ATT_DH

    def put_kv(rows, kv, rope_tabs):
        k = kv[:, :LANES].astype(F32)
        v = kv[:, LANES:].astype(F32)
        k = k * lax.rsqrt(_group_ms64(k) + NORM_EPS) * kg_ref[...]
        if rope_tabs is not None:
            k = _rope128(k, *rope_tabs)
        k_sw = pltpu.roll(k, ATT_DH, axis=1)
        v_sw = pltpu.roll(v, ATT_DH, axis=1)
        kx[0, rows, :] = jnp.where(lo_half, k, k_sw).astype(BF16)
        kx[1, rows, :] = jnp.where(lo_half, k_sw, k).astype(BF16)
        rid = lax.broadcasted_iota(jnp.int32, (ATT_V_ROWS, 1), 0)
        for g, vg in enumerate((v, v_sw)):
            vt = vg.T[:ATT_V_ROWS, :]
            vx[g, :, rows] = jnp.where(rid < ATT_DH, vt, jnp.where(rid == ATT_DH, 1.0, 0.0)).astype(BF16)

    @pl.when(i == 0)
    def _():
        if T:
            def body(j, carry):
                rows = pl.ds(pl.multiple_of(j * tk, tk), tk)
                put_kv(rows, kvl_ref[rows, :], (ck_ref[rows, :], s1k_ref[rows, :], s2k_ref[rows, :]))
                return carry
            lax.fori_loop(0, T // tk, body, 0)
        put_kv(pl.ds(T, C), kvc_ref[...], None)

    n_slab = ATT_Q // LANES
    tq = q_ref.shape[0]
    for s in range(n_slab):
        q = q_ref[:, s * LANES:(s + 1) * LANES].astype(F32)
        q = q * lax.rsqrt(_group_ms64(q) + NORM_EPS) * qg_ref[...]
        if use_rope:
            q = _rope128(q, cq_ref[...], s1q_ref[...], s2q_ref[...])
        q = (q * (ATT_DH ** -0.5 * LOG2_E)).astype(BF16)
        qh[2 * s] = jnp.where(lo_half, q, jnp.zeros_like(q))
        qh[2 * s + 1] = jnp.where(lo_half, jnp.zeros_like(q), q)

    heads_per_kv = ATT_HEADS // ATT_KV_HEADS
    lo_rows = lax.broadcasted_iota(jnp.int32, (LANES, 1), 0) < ATT_DH
    for g in range(ATT_KV_HEADS):
        heads = range(g * heads_per_kv, (g + 1) * heads_per_kv)

        ahead = ATT_SCORE_LOOKAHEAD

        def chunk(rows, carry):
            kc = kx[g, rows, :]
            vt = vx[g, :, rows]
            scores = {h: _dot_nt(kc, qh[h]) for h in heads[:ahead]}
            new = []
            for n, (h, (m, acc)) in enumerate(zip(heads, carry)):
                if n + ahead < len(heads):
                    scores[heads[n + ahead]] = _dot_nt(kc, qh[heads[n + ahead]])
                sc = scores.pop(h)
                m_new = jnp.maximum(m, jnp.max(sc, axis=0, keepdims=True))
                p = jnp.exp2(sc - m_new).astype(BF16)
                acc = jnp.exp2(m - m_new) * acc + _dot(vt, p)
                new.append((m_new, acc))
            return tuple(new)

        init = (jnp.full((1, tq), -jnp.inf, F32), jnp.zeros((ATT_V_ROWS, tq), F32))
        carry = (init,) * heads_per_kv
        if T:
            carry = lax.fori_loop(
                0, T // tk, lambda j, cr: chunk(pl.ds(pl.multiple_of(j * tk, tk), tk), cr), carry)
        carry = chunk(pl.ds(T, C), carry)
        for hi in range(0, heads_per_kv, 2):
            s = (g * heads_per_kv + hi) // 2
            halves = [a[:ATT_DH, :] / a[ATT_DH:ATT_DH + 1, :] for _, a in (carry[hi], carry[hi + 1])]
            o_ref[:, s * LANES:(s + 1) * LANES] = jnp.concatenate(halves, axis=0).T.astype(BF16)


def _att_call(u, q_row_off, B, Tq, T, C, ctx_row_off, tabs, qg, kg):
    tq = min(TQ_ATT, Tq)
    nq = Tq // tq
    tk = TK_ATT
    S = T + C
    hpk = ATT_HEADS // ATT_KV_HEADS
    cq, ckv = OFF_AQ // 512, OFF_AKV // 256
    qoff = q_row_off // tq
    coff = ctx_row_off // C
    g_spec = pl.BlockSpec((1, LANES), lambda b, i: (0, 0))
    q_spec = pl.BlockSpec((tq, ATT_Q), lambda b, i: (qoff + b * nq + i, cq))
    kvc_spec = pl.BlockSpec((C, 2 * ATT_KV), lambda b, i: (coff + b, ckv))
    if T:
        tq_tab = pl.BlockSpec((tq, LANES), lambda b, i: (i, 0))
        tk_tab = pl.BlockSpec((T, LANES), lambda b, i: (0, 0))
        in_specs = [q_spec, pl.BlockSpec((T, 2 * ATT_KV), lambda b, i: (b, ckv)), kvc_spec,
                    tq_tab, tq_tab, tq_tab, tk_tab, tk_tab, tk_tab, g_spec, g_spec]
        args = (u, u, u, tabs[0], tabs[1], tabs[2], tabs[0], tabs[1], tabs[2], qg, kg)
    else:
        in_specs = [q_spec, kvc_spec, g_spec, g_spec]
        args = (u, u, qg, kg)
    return pl.pallas_call(
        functools.partial(_att_kernel, T=T, C=C, tk=tk, use_rope=bool(T)),
        grid=(B, nq),
        in_specs=in_specs,
        out_specs=pl.BlockSpec((tq, ATT_Q), lambda b, i: (b * nq + i, 0)),
        out_shape=jax.ShapeDtypeStruct((B * Tq, ATT_Q), BF16),
        scratch_shapes=[pltpu.VMEM((ATT_KV_HEADS, S, LANES), BF16), pltpu.VMEM((ATT_KV_HEADS, ATT_V_ROWS, S), BF16),
                        pltpu.VMEM((ATT_HEADS, tq, LANES), BF16)],
        compiler_params=pltpu.CompilerParams(dimension_semantics=("arbitrary", "arbitrary"),
                                             vmem_limit_bytes=VMEM_LIMIT_BYTES, flags=ATT_FLAGS),
        name="gqa_lat" if T else "gqa_ctx",
    )(*args)


def _merge_kernel(*refs, n_seq_tiles, fill_tail, **kw):
    xo_ref, h2_ref, lg_ref = refs[-3:]
    if not fill_tail:
        _merge_body(*refs, **kw)
        return
    i = pl.program_id(0)

    @pl.when(i < n_seq_tiles)
    def _():
        _merge_body(*refs, **kw)

    @pl.when(i >= n_seq_tiles)
    def _():
        for r in (xo_ref, h2_ref, lg_ref):
            r[...] = jnp.zeros_like(r)


def _merge_body(*refs, T, tm, tiles_per_batch, row_base):
    (of_ref, ob_ref, gr_ref, pu_ref, pp_ref, pn_ref, gt_ref, ya_ref, x_ref, g1_ref, sh2_ref, sc2_ref,
     gng_ref, band_ref, pw_ref, ps_ref, wb_ref, wo_ref, n2g_ref, wrh_ref, wrl_ref, br_ref) = refs[:22]
    xo_ref, h2_ref, lg_ref = refs[-3:]
    i = pl.program_id(0)
    it = i % tiles_per_batch
    b = i // tiles_per_batch if row_base is None else row_base

    gt = gt_ref[...].astype(F32)
    z = _sigmoid(gt[:, D_MODEL:2 * D_MODEL]) * _dot(ya_ref[...], wb_ref[1])

    prev = jnp.where(it > 0, pp_ref[...], jnp.zeros_like(pp_ref[...]))
    nxt = jnp.where(it < tiles_per_batch - 1, pn_ref[...], jnp.zeros_like(pn_ref[...]))
    ext = jnp.concatenate([prev, pu_ref[...], nxt], axis=0)
    sub = 128
    n_sub = tm // sub
    ext_sub = [ext[r * sub:r * sub + sub + 2 * POOL_HALO, :] for r in range(n_sub)]
    groups = [slice(gi * POOL_GROUP, (gi + 1) * POOL_GROUP) for gi in range(len(POOL_WINDOWS))]
    wsums = [[_dot(band_ref[gi], e[:, gs]) for gi, gs in enumerate(groups)] for e in ext_sub]
    yp_rows = []
    for r, e in enumerate(ext_sub):
        t = it * tm + r * sub + lax.broadcasted_iota(jnp.int32, (sub, 1), 0)
        cols = []
        for gi, win in enumerate(POOL_WINDOWS):
            cnt = (jnp.minimum(t + win // 2, T) - jnp.maximum(t - win // 2, 0)).astype(F32)
            d = wsums[r][gi] / cnt - e[POOL_HALO:POOL_HALO + sub, groups[gi]].astype(F32)
            cols.append(_dot(d.astype(BF16), pw_ref[gi]))
        yp_rows.append(jnp.concatenate(cols, axis=1))
    y_pool = (jnp.concatenate(yp_rows, axis=0) * ps_ref[...]).astype(BF16)
    z = z + _sigmoid(gt[:, 2 * D_MODEL:]) * _dot(y_pool, wb_ref[2])

    o = of_ref[...] + ob_ref[...]
    gr = gr_ref[...].astype(F32)
    parts = []
    for h in range(GLA_HEADS):
        hs = slice(h * GLA_DV, (h + 1) * GLA_DV)
        oh = o[:, hs]
        ms = jnp.mean(oh * oh, axis=-1, keepdims=True)
        parts.append(oh * lax.rsqrt(ms + NORM_EPS) * gng_ref[...])
    y_gla = (jnp.concatenate(parts, axis=1) * _silu(gr)).astype(BF16)
    z = z + _sigmoid(gt[:, :D_MODEL]) * _dot(y_gla, wb_ref[0])
    y = _dot(z.astype(BF16), wo_ref[...])
    xn = x_ref[...] + g1_ref[pl.ds(b, 1), :] * y
    xo_ref[...] = xn
    ms = jnp.mean(xn * xn, axis=-1, keepdims=True)
    h2 = xn * lax.rsqrt(ms + NORM_EPS) * n2g_ref[...]
    h2 = h2 * (1.0 + sc2_ref[pl.ds(b, 1), :]) + sh2_ref[pl.ds(b, 1), :]
    _store_token_rows(h2_ref, h2)
    hh, hl = _split_bf16(h2)
    lg_ref[...] = _dot(hh, wrh_ref[...]) + _dot(hh, wrl_ref[...]) + _dot(hl, wrh_ref[...]) + br_ref[...]


def _merge_call(o_f, o_b, u, y_att, xin, x_row_off, B, T, mods, l, mod_row, wts, n_out_rows, prev_outs):
    tm = min(TM_MERGE, T)
    tpb = T // tm
    D = D_MODEL
    R = mods.shape[2]
    ro = x_row_off // tm
    ro16 = x_row_off // POOL_HALO
    r16 = tm // POOL_HALO
    n16 = u.shape[0] // POOL_HALO
    cgr, cpu = OFF_GR // 512, OFF_PU // 512
    n_seq_tiles = B * tpb
    n_grid = n_seq_tiles if prev_outs is not None else (n_out_rows - x_row_off) // tm
    fill_tail = n_grid > n_seq_tiles
    ic = lambda i: jnp.minimum(i, n_seq_tiles - 1)
    full = lambda shp: pl.BlockSpec(shp, lambda i: (0,) * len(shp))
    mod_spec = lambda k: pl.BlockSpec((None, None, R, D), lambda i: (l, k, 0, 0))
    in_specs = [
        pl.BlockSpec((tm, GLA_V), lambda i: (ic(i), 0)),
        pl.BlockSpec((tm, GLA_V), lambda i: (ic(i), 0)),
        pl.BlockSpec((tm, 512), lambda i: (ro + ic(i), cgr)),
        pl.BlockSpec((tm, 512), lambda i: (ro + ic(i), cpu)),
        pl.BlockSpec((POOL_HALO, 512), lambda i: (jnp.maximum(ro16 + ic(i) * r16 - 1, 0), cpu)),
        pl.BlockSpec((POOL_HALO, 512), lambda i: (jnp.minimum(ro16 + (ic(i) + 1) * r16, n16 - 1), cpu)),
        pl.BlockSpec((tm, N_BRANCH * D), lambda i: (ro + ic(i), 0)),
        pl.BlockSpec((tm, ATT_Q), lambda i: (ic(i), 0)),
        pl.BlockSpec((tm, D), lambda i: (ro + ic(i), 0)),
        mod_spec(2), mod_spec(3), mod_spec(4),
        full((1, GLA_DV)), full((4, 128, 128 + 2 * POOL_HALO)), full((4, POOL_GROUP, POOL_GROUP)), full((1, POOL_WIDTH)),
        full((N_BRANCH, BRANCH_WIDTH, D)), full((D, D)), full((1, D)), full((D, ROUTER_W)), full((D, ROUTER_W)),
        full((1, ROUTER_W)),
    ]
    args = [o_f, o_b, u, u, u, u, u, y_att, xin, mods, mods, mods, *wts]
    out_shape = [jax.ShapeDtypeStruct((n_out_rows, D), F32), jax.ShapeDtypeStruct((n_out_rows * TOK_ROWS, LANES), F32),
                 jax.ShapeDtypeStruct((n_out_rows, ROUTER_W), F32)]
    out_specs = [pl.BlockSpec((tm, D), lambda i: (ro + i, 0)), pl.BlockSpec((tm * TOK_ROWS, LANES), lambda i: (ro + i, 0)),
                 pl.BlockSpec((tm, ROUTER_W), lambda i: (ro + i, 0))]
    aliases = {}
    if prev_outs is not None:
        n_in = len(args)
        in_specs += [pl.BlockSpec(memory_space=pl.ANY)] * 3
        args += list(prev_outs)
        aliases = {n_in: 0, n_in + 1: 1, n_in + 2: 2}
    kern = functools.partial(_merge_kernel, n_seq_tiles=n_seq_tiles, fill_tail=fill_tail,
                             T=T, tm=tm, tiles_per_batch=tpb, row_base=mod_row)
    return pl.pallas_call(
        kern,
        grid=(n_grid,),
        in_specs=in_specs,
        out_specs=out_specs,
        out_shape=out_shape,
        input_output_aliases=aliases,
        compiler_params=_cparams(("arbitrary",)),
        name="merge",
    )(*args)


def _router_kernel(lg_ref, out_ref, cnt_ref, carry, pstart, *, tm, bm):
    ph = pl.program_id(0)
    i = pl.program_id(1)
    lane = lax.broadcasted_iota(jnp.int32, (1, LANES), 1)
    neg = -jnp.inf
    x = lg_ref[...]

    @pl.when((ph == 0) & (i == 0))
    def _():
        carry[...] = jnp.zeros_like(carry)

    gl = jnp.where(lane < N_GROUPS, x, neg)
    gmax = jnp.max(gl, axis=-1, keepdims=True)
    gsum = jnp.sum(jnp.exp(gl - gmax), axis=-1, keepdims=True)
    grp = jnp.min(jnp.where(gl == gmax, lane, LANES), axis=-1, keepdims=True)
    e_lane = lane - N_GROUPS
    lane_grp = sum((e_lane >= EXP_PER_GROUP * k).astype(jnp.int32) for k in range(1, N_GROUPS))
    lane_grp = jnp.where(e_lane < 0, -1, jnp.where(e_lane < N_EXPERTS, lane_grp, -1))
    in_grp = lane_grp == grp
    el = jnp.where(in_grp, x, neg)
    emax = jnp.max(el, axis=-1, keepdims=True)
    esum = jnp.sum(jnp.exp(el - emax), axis=-1, keepdims=True)
    i1 = jnp.min(jnp.where(el == emax, lane, LANES), axis=-1, keepdims=True)
    el2 = jnp.where(lane == i1, neg, el)
    m2 = jnp.max(el2, axis=-1, keepdims=True)
    i2 = jnp.min(jnp.where(el2 == m2, lane, LANES), axis=-1, keepdims=True)
    p1 = 1.0 / esum
    p2 = jnp.exp(m2 - emax) / esum
    pg = 1.0 / gsum
    w1 = pg * p1 / (p1 + p2)
    w2 = pg * p2 / (p1 + p2)
    oh1 = lane == i1
    oh2 = lane == i2
    oh = jnp.where(oh1, 1.0, jnp.where(oh2, 1.0, 0.0))

    @pl.when(ph == 0)
    def _():
        carry[...] += jnp.sum(oh, axis=0, keepdims=True)

    @pl.when((ph == 1) & (i == 0))
    def _():
        cnt = carry[...]
        cnt_ref[...] = jnp.broadcast_to(cnt, cnt_ref.shape)
        nb = jnp.floor((cnt + (bm - 1)) * (1.0 / bm))
        hi = jnp.floor(nb * (1.0 / 16.0))
        lo = nb - 16.0 * hi
        r = lax.broadcasted_iota(jnp.int32, (LANES, LANES), 0)
        c = lax.broadcasted_iota(jnp.int32, (LANES, LANES), 1)
        upper = jnp.where(r < c, 1.0, 0.0).astype(BF16)
        hi8 = jnp.broadcast_to(hi, (8, LANES)).astype(BF16)
        lo8 = jnp.broadcast_to(lo, (8, LANES)).astype(BF16)
        pre = 16.0 * _dot(hi8, upper) + _dot(lo8, upper)
        pstart[...] = pre[0:1, :] * bm
        carry[...] = jnp.zeros_like(carry)

    @pl.when(ph == 1)
    def _():
        r = lax.broadcasted_iota(jnp.int32, (tm, tm), 0)
        c = lax.broadcasted_iota(jnp.int32, (tm, tm), 1)
        lower = jnp.where(c < r, 1.0, 0.0).astype(BF16)
        before = _dot(lower, oh.astype(BF16))
        slot = pstart[...] + carry[...] + before
        d1 = jnp.sum(jnp.where(oh1, slot, 0.0), axis=-1, keepdims=True)
        d2 = jnp.sum(jnp.where(oh2, slot, 0.0), axis=-1, keepdims=True)
        carry[...] += jnp.sum(oh, axis=0, keepdims=True)
        out_ref[...] = jnp.where(lane == 0, d1, jnp.where(lane == 1, d2, jnp.where(lane == 2, w1, jnp.where(lane == 3, w2, 0.0))))


def _router_call(logits, bm):
    N = logits.shape[0]
    tm = TM_ROUTE
    return pl.pallas_call(
        functools.partial(_router_kernel, tm=tm, bm=bm),
        grid=(2, N // tm),
        in_specs=[pl.BlockSpec((tm, ROUTER_W), lambda p, i: (i, 0))],
        out_specs=[pl.BlockSpec((tm, LANES), lambda p, i: (i * p, 0)), pl.BlockSpec((8, LANES), lambda p, i: (0, 0))],
        out_shape=[jax.ShapeDtypeStruct((N, LANES), F32), jax.ShapeDtypeStruct((8, LANES), F32)],
        scratch_shapes=[pltpu.VMEM((1, LANES), F32), pltpu.VMEM((1, LANES), F32)],
        compiler_params=_cparams(("arbitrary", "arbitrary")),
        name="router",
    )(logits)


def _by_parity(i, fn):
    for p in range(2):
        pl.when(i % 2 == p)(functools.partial(fn, p))


def _dispatch_kernel(idx_hbm, h2_ref, xs_in, xs_hbm, idx_a, idx_b, isem, dsem, *, tm, n_tiles):
    del xs_in
    i = pl.program_id(0)
    idx_s = (idx_a, idx_b)

    def idx_copy(j, slot):
        return pltpu.make_async_copy(idx_hbm.at[j], idx_s[slot], isem.at[slot])

    def row_copy(r, slot):
        return pltpu.make_async_copy(h2_ref.at[r], xs_hbm.at[slot], dsem)

    @pl.when(i == 0)
    def _():
        idx_copy(0, 0).start()

    def step(p):
        @pl.when(i + 1 < n_tiles)
        def _():
            idx_copy(i + 1, 1 - p).start()

        idx_copy(i, p).wait()

        def issue(r0, c):
            for u in range(DMA_UNROLL):
                r = r0 * DMA_UNROLL + u
                for k in range(TOP_K):
                    row_copy(r, idx_s[p][0, TOP_K * r + k]).start()
            return c

        lax.fori_loop(0, tm // DMA_UNROLL, issue, 0)

    _by_parity(i, step)

    def drain(r0, c):
        for _ in range(TOP_K * DMA_UNROLL):
            row_copy(0, 0).wait()
        return c

    lax.fori_loop(0, tm // DMA_UNROLL, drain, 0)


def _dispatch_call(idx, h2, n_slots):
    n_tiles = idx.shape[0]
    tm = TM_DISP
    xs0 = jnp.zeros((n_slots, TOK_ROWS, LANES), F32)
    return pl.pallas_call(
        functools.partial(_dispatch_kernel, tm=tm, n_tiles=n_tiles),
        grid=(n_tiles,),
        in_specs=[pl.BlockSpec(memory_space=pl.ANY),
                  pl.BlockSpec((tm, TOK_ROWS, LANES), lambda i: (i, 0, 0)),
                  pl.BlockSpec(memory_space=pl.ANY)],
        out_specs=pl.BlockSpec(memory_space=pl.ANY),
        out_shape=jax.ShapeDtypeStruct((n_slots, TOK_ROWS, LANES), F32),
        input_output_aliases={2: 0},
        scratch_shapes=[pltpu.SMEM((1, TOP_K * tm), jnp.int32), pltpu.SMEM((1, TOP_K * tm), jnp.int32),
                        pltpu.SemaphoreType.DMA((2,)),
                        pltpu.SemaphoreType.DMA(())],
        compiler_params=pltpu.CompilerParams(dimension_semantics=("arbitrary",), vmem_limit_bytes=VMEM_LIMIT_BYTES,
                                             has_side_effects=True),
        name="dispatch",
    )(idx, h2, xs0)


def _expert_kernel(be_ref, nu_ref, xs_ref, wg_ref, wu_ref, wd_ref, ys_ref, wgb, wub, wdb, *, bm):
    i = pl.program_id(0)

    @pl.when((i == 0) | (be_ref[i] != be_ref[jnp.maximum(i - 1, 0)]))
    def _():
        wgb[...] = wg_ref[...].astype(BF16)
        wub[...] = wu_ref[...].astype(BF16)
        wdb[...] = wd_ref[...].astype(BF16)

    @pl.when(i < nu_ref[0])
    def _():
        xb = _load_token_rows(xs_ref, bm).astype(BF16)
        g = _dot(xb, wgb[...])
        up = _dot(xb, wub[...])
        hmid = (_silu(g) * up).astype(BF16)
        _store_token_rows(ys_ref, _dot(hmid, wdb[...]))

    @pl.when(i >= nu_ref[0])
    def _():
        ys_ref[...] = jnp.zeros_like(ys_ref)


def _expert_call(blk_expert, nused, xs, wg, wu, wd, l):
    bm = BM_MOE
    D = D_MODEL
    nblk = blk_expert.shape[0]
    w_in_spec = pl.BlockSpec((None, None, D, D_EXPERT), lambda i, be, nu: (l, be[i], 0, 0))
    grid_spec = pltpu.PrefetchScalarGridSpec(
        num_scalar_prefetch=2,
        grid=(nblk,),
        in_specs=[
            pl.BlockSpec((bm * TOK_ROWS, LANES), lambda i, be, nu: (jnp.minimum(i, nu[0] - 1), 0)),
            w_in_spec, w_in_spec,
            pl.BlockSpec((None, None, D_EXPERT, D), lambda i, be, nu: (l, be[i], 0, 0)),
        ],
        out_specs=pl.BlockSpec((bm * TOK_ROWS, LANES), lambda i, be, nu: (i, 0)),
        scratch_shapes=[pltpu.VMEM((D, D_EXPERT), BF16), pltpu.VMEM((D, D_EXPERT), BF16),
                        pltpu.VMEM((D_EXPERT, D), BF16)],
    )
    return pl.pallas_call(
        functools.partial(_expert_kernel, bm=bm),
        grid_spec=grid_spec,
        out_shape=jax.ShapeDtypeStruct(xs.shape, F32),
        compiler_params=_cparams(("arbitrary",)),
        name="experts",
    )(blk_expert, nused, xs, wg, wu, wd)


def _combine_kernel(idx_hbm, ys_hbm, x_ref, r_ref, g2_ref, fg_ref, o_ref, idx_a, idx_b, isem, ybuf, gsem, *,
                    tm, n_tiles, n_lat_tiles, tiles_per_batch, ctx_row, final):
    i = pl.program_id(0)
    idx_s = (idx_a, idx_b)

    def idx_copy(j, slot):
        return pltpu.make_async_copy(idx_hbm.at[j], idx_s[slot], isem.at[slot])

    def row_copy(slot, r, k, src):
        rows = pl.ds(pl.multiple_of((k * tm + r) * TOK_ROWS, TOK_ROWS), TOK_ROWS)
        return pltpu.make_async_copy(ys_hbm.at[src], ybuf.at[slot, rows, :], gsem.at[slot])

    def start_gather(slot):
        def issue(r0, c):
            for u in range(DMA_UNROLL):
                r = r0 * DMA_UNROLL + u
                for k in range(TOP_K):
                    row_copy(slot, r, k, idx_s[slot][0, TOP_K * r + k]).start()
            return c
        lax.fori_loop(0, tm // DMA_UNROLL, issue, 0)

    def wait_gather(slot):
        def drain(r0, c):
            for _ in range(TOP_K * DMA_UNROLL):
                row_copy(slot, 0, 0, 0).wait()
            return c
        lax.fori_loop(0, tm // DMA_UNROLL, drain, 0)

    @pl.when(i == 0)
    def _():
        idx_copy(0, 0).start()
        idx_copy(0, 0).wait()
        start_gather(0)
        if n_tiles > 1:
            idx_copy(1, 1).start()

    def step(p):
        @pl.when(i + 1 < n_tiles)
        def _():
            idx_copy(i + 1, 1 - p).wait()
            start_gather(1 - p)

        @pl.when(i + 2 < n_tiles)
        def _():
            idx_copy(i + 2, p).start()

        wait_gather(p)
        yb = ybuf.at[p]
        y0 = _load_token_rows(yb, tm, 0)
        y1 = _load_token_rows(yb, tm, tm * TOK_ROWS)
        m = r_ref[:, 2:3] * y0 + r_ref[:, 3:4] * y1
        b = jnp.where(i < n_lat_tiles, i // tiles_per_batch, ctx_row)
        xn = x_ref[...] + g2_ref[pl.ds(b, 1), :] * m
        if final:
            ms = jnp.mean(xn * xn, axis=-1, keepdims=True)
            xn = xn * lax.rsqrt(ms + NORM_EPS) * fg_ref[...]
        o_ref[...] = xn

    _by_parity(i, step)


def _combine_call(idx, ys, x_mid, route, mods, l, final_g, n_lat_rows, rows_per_batch, ctx_row, final):
    tm = TM_COMB
    D = D_MODEL
    R = mods.shape[2]
    n_rows = x_mid.shape[0]
    n_tiles = n_rows // tm
    kern = functools.partial(_combine_kernel, tm=tm, n_tiles=n_tiles, n_lat_tiles=n_lat_rows // tm,
                             tiles_per_batch=rows_per_batch // tm, ctx_row=ctx_row, final=final)
    return pl.pallas_call(
        kern,
        grid=(n_tiles,),
        in_specs=[
            pl.BlockSpec(memory_space=pl.ANY),
            pl.BlockSpec(memory_space=pl.ANY),
            pl.BlockSpec((tm, D), lambda i: (i, 0)),
            pl.BlockSpec((tm, LANES), lambda i: (i, 0)),
            pl.BlockSpec((None, None, R, D), lambda i: (l, 5, 0, 0)),
            pl.BlockSpec((1, D), lambda i: (0, 0)),
        ],
        out_specs=pl.BlockSpec((tm, D), lambda i: (i, 0)),
        out_shape=jax.ShapeDtypeStruct((n_rows, D), F32),
        scratch_shapes=[pltpu.SMEM((1, TOP_K * tm), jnp.int32), pltpu.SMEM((1, TOP_K * tm), jnp.int32),
                        pltpu.SemaphoreType.DMA((2,)),
                        pltpu.VMEM((2, TOP_K * tm * TOK_ROWS, LANES), F32), pltpu.SemaphoreType.DMA((2,))],
        compiler_params=_cparams(("arbitrary",)),
        name="combine",
    )(idx, ys, x_mid, route, mods, final_g.reshape(1, D))


def _permute_w_in(w_in):
    gq, gk, gv, gr, glf, glb, aq, ak, av, pu, gt = jnp.split(w_in, np.cumsum(IN_SIZES)[:-1].tolist(), axis=-1)
    pad = jnp.zeros(w_in.shape[:-1] + (U_WIDTH - sum(IN_SIZES),), w_in.dtype)
    return jnp.concatenate([gt, gq, gk, gv, gr, aq, pu, ak, av, glf, glb, pad], axis=-1).astype(BF16)


def _rope_tables(T):
    rows = T // GRID_W
    row = np.repeat(np.arange(rows), GRID_W).astype(np.float32)
    col = np.tile(np.arange(GRID_W), rows).astype(np.float32)
    inv = jnp.asarray(ROPE_THETA, F32) ** (-jnp.arange(0, ROPE_AXIS_DIM, 2, dtype=F32) / ROPE_AXIS_DIM)
    ang_r = jnp.asarray(row)[:, None] * inv
    ang_c = jnp.asarray(col)[:, None] * inv
    zero = jnp.zeros_like(ang_r)
    cos = jnp.concatenate([jnp.cos(ang_r)] * 2 + [jnp.cos(ang_c)] * 2, axis=1)
    s1 = jnp.concatenate([-jnp.sin(ang_r), zero, -jnp.sin(ang_c), zero], axis=1)
    s2 = jnp.concatenate([zero, jnp.sin(ang_r), zero, jnp.sin(ang_c)], axis=1)
    return tuple(jnp.concatenate([t, t], axis=1) for t in (cos, s1, s2))


def _pool_bands():
    i = np.arange(128)[:, None]
    j = np.arange(128 + 2 * POOL_HALO)[None, :]
    bands = [((j >= i + POOL_HALO - w // 2) & (j < i + POOL_HALO + w // 2)).astype(np.float32) for w in POOL_WINDOWS]
    return jnp.asarray(np.stack(bands), BF16)


def _block_table(counts, n_tok, bm):
    cnt = counts[0, N_GROUPS:N_GROUPS + N_EXPERTS].astype(jnp.int32)
    pad_end = jnp.cumsum((cnt + bm - 1) // bm * bm)
    nblk = -(-(n_tok * TOP_K + N_EXPERTS * (bm - 1)) // bm)
    blk_start = jnp.arange(nblk, dtype=jnp.int32) * bm
    blk_expert = jnp.minimum(jnp.sum((pad_end[None, :] <= blk_start[:, None]).astype(jnp.int32), axis=1), N_EXPERTS - 1)
    nused = (pad_end[-1] // bm).astype(jnp.int32).reshape(1)
    return blk_expert.astype(jnp.int32), nused, nblk


def _slot_tiles(route, tm):
    n = route.shape[0]
    return route[:, :TOP_K].astype(jnp.int32).reshape(n // tm, 1, TOP_K * tm)


def kernel(x, c, ctx, c_ctx, w_mod, b_mod, norm1_g, norm2_g, w_in, gla_a_up_f, gla_a_bias_f, gla_a_up_b, gla_a_bias_b, gla_norm_g, att_qn_g, att_kn_g, pool_w, pool_scale, w_branch, w_out, moe_w_group, moe_b_group, moe_w_expert, moe_b_expert, moe_w_gate, moe_w_up, moe_w_down, final_g):
    B, T, D = x.shape
    C = ctx.shape[1]
    L = w_mod.shape[0]
    n_lat, n_ctx = B * T, B * C
    MOD_ROWS = 16
    assert D == D_MODEL and B < MOD_ROWS and T % TQ_ATT == 0 and C % GLA_CHUNK == 0

    s_in = jnp.concatenate([c, c_ctx[None], jnp.zeros((MOD_ROWS - B - 1, D), F32)], axis=0)
    mods = _mod_call(s_in, w_mod, b_mod)
    w_perm = _permute_w_in(w_in)
    tabs = _rope_tables(T)
    bands = _pool_bands()
    zero_state = jnp.zeros((B, GLA_V, GLA_QK), F32)

    xall = jnp.concatenate([x.reshape(n_lat, D), ctx.reshape(n_ctx, D)], axis=0)
    out = None
    for l in range(L):
        want_ctx = l < L - 1
        n_rows = xall.shape[0]
        u = _inproj_call(xall, mods, l, norm1_g[l], w_perm[l], n_lat, T, B)

        up = jnp.zeros((128, 2 * GLA_QK), F32)
        up = up.at[:GLA_RANK, :GLA_QK].set(gla_a_up_f[l]).at[GLA_RANK:2 * GLA_RANK, GLA_QK:].set(gla_a_up_b[l]).astype(BF16)
        bias = jnp.concatenate([gla_a_bias_f[l], gla_a_bias_b[l]]).reshape(1, 2 * GLA_QK)
        ofc, obc, sfc, sbc = _gla_call(u, n_lat, B, C, zero_state, zero_state, up, bias)
        of, ob, _, _ = _gla_call(u, 0, B, T, sfc, sbc, up, bias)

        qg = jnp.tile(att_qn_g[l], 2).reshape(1, LANES)
        kg = jnp.tile(att_kn_g[l], 2).reshape(1, LANES)
        ya = _att_call(u, 0, B, T, T, C, n_lat, tabs, qg, kg)

        wr = jnp.zeros((D, ROUTER_W), F32).at[:, :N_GROUPS].set(moe_w_group[l]).at[:, N_GROUPS:N_GROUPS + N_EXPERTS].set(moe_w_expert[l])
        wrh, wrl = _split_bf16(wr)
        br = jnp.zeros((1, ROUTER_W), F32).at[0, :N_GROUPS].set(moe_b_group[l]).at[0, N_GROUPS:N_GROUPS + N_EXPERTS].set(moe_b_expert[l])
        wts = (gla_norm_g[l].reshape(1, GLA_DV), bands, pool_w[l].astype(BF16), pool_scale[l].reshape(1, POOL_WIDTH),
               w_branch[l].astype(BF16), w_out[l].astype(BF16), norm2_g[l].reshape(1, D), wrh, wrl, br)
        n_tok = n_rows if want_ctx else n_lat
        outs = _merge_call(of, ob, u, ya, xall, 0, B, T, mods, l, None, wts, n_tok, None)
        if want_ctx:
            yac = _att_call(u, n_lat, B, C, 0, C, n_lat, None, qg, kg)
            outs = _merge_call(ofc, obc, u, yac, xall, n_lat, B, C, mods, l, B, wts, n_tok, outs)
        x_mid, h2, logits = outs

        route, counts = _router_call(logits, BM_MOE)
        blk_expert, nused, nblk = _block_table(counts, n_tok, BM_MOE)
        n_slots = nblk * BM_MOE
        xs = _dispatch_call(_slot_tiles(route, TM_DISP), h2.reshape(n_tok, TOK_ROWS, LANES), n_slots)
        ys = _expert_call(blk_expert, nused, xs.reshape(n_slots * TOK_ROWS, LANES), moe_w_gate, moe_w_up, moe_w_down, l)
        xall = _combine_call(_slot_tiles(route, TM_COMB), ys.reshape(n_slots, TOK_ROWS, LANES), x_mid, route, mods, l,
                             final_g, n_lat, T, B, final=not want_ctx)
    return xall[:n_lat].reshape(B, T, D)
```

```python
import functools

import numpy as np
import jax
import jax.numpy as jnp
from jax import lax
from jax.experimental import pallas as pl
from jax.experimental.pallas import tpu as pltpu

F32 = jnp.float32
BF16 = jnp.bfloat16

VMEM_LIMIT_BYTES = 56 * 1024 * 1024
LANES = 128

D_MODEL = 1024
GRID_W = 64
NORM_EPS = 1e-6
GLA_HEADS, GLA_DK, GLA_DV, GLA_RANK, GLA_TAU, GLA_CHUNK = 4, 64, 128, 16, 16.0, 64
GLA_QK, GLA_V = GLA_HEADS * GLA_DK, GLA_HEADS * GLA_DV
ATT_HEADS, ATT_KV_HEADS, ATT_DH = 8, 2, 64
ROPE_THETA, ROPE_AXIS_DIM = 10000.0, 32
ATT_Q, ATT_KV = ATT_HEADS * ATT_DH, ATT_KV_HEADS * ATT_DH
POOL_WINDOWS, POOL_GROUP = (2, 4, 8, 16), 128
POOL_WIDTH = POOL_GROUP * len(POOL_WINDOWS)
POOL_HALO = 16
N_BRANCH, BRANCH_WIDTH = 3, 512
N_GROUPS, EXP_PER_GROUP, TOP_K, D_EXPERT = 4, 8, 2, 512
N_EXPERTS = N_GROUPS * EXP_PER_GROUP
IN_SIZES = (GLA_QK, GLA_QK, GLA_V, GLA_V, GLA_RANK, GLA_RANK, ATT_Q, ATT_KV, ATT_KV, POOL_WIDTH, N_BRANCH * D_MODEL)

OFF_GT, OFF_GQK, OFF_GV, OFF_GR, OFF_AQ, OFF_PU, OFF_AKV, OFF_GL = 0, 3072, 3584, 4096, 4608, 5120, 5632, 5888
U_WIDTH = 6144
U_CHUNK = 512

TM_IN = 512
TB_GLA = 256
GLA_CHUNKS_PER_BODY = 4
TQ_ATT = 1024
TK_ATT = 512
ATT_SUB_ROWS = 64
LOG2_E = 1.4426950408889634
ATT_FLAGS = {}
ATT_V_ROWS = 80
ATT_SCORE_LOOKAHEAD = 2
TM_MERGE = 512
BM_MOE = 256
TM_ROUTE = 512
TM_DISP = 256
TM_COMB = 256
ROUTER_W = 128
TOK_ROWS = D_MODEL // LANES
DMA_UNROLL = 8


def _cparams(sem):
    return pltpu.CompilerParams(dimension_semantics=sem, vmem_limit_bytes=VMEM_LIMIT_BYTES)


def _split_bf16(a):
    hi = a.astype(BF16)
    lo = (a - hi.astype(F32)).astype(BF16)
    return hi, lo


def _dot(a, b):
    return jnp.dot(a, b, preferred_element_type=F32)


def _dot_nt(a, b):
    return lax.dot_general(a, b, (((1,), (1,)), ((), ())), preferred_element_type=F32)


def _dot_tn(a, b):
    return lax.dot_general(a, b, (((0,), (0,)), ((), ())), preferred_element_type=F32)


def _dot3(a, b):
    ah, al = _split_bf16(a)
    bh, bl = _split_bf16(b)
    return _dot(ah, bh) + _dot(ah, bl) + _dot(al, bh)


def _load_token_rows(ref, n_tok, row0=0):
    return jnp.concatenate([ref[pl.ds(row0 + s, n_tok, stride=TOK_ROWS), :] for s in range(TOK_ROWS)], axis=1)


def _store_token_rows(ref, val):
    n_tok = val.shape[0]
    for s in range(TOK_ROWS):
        ref[pl.ds(s, n_tok, stride=TOK_ROWS), :] = val[:, s * LANES:(s + 1) * LANES]


def _sigmoid(x):
    return 1.0 / (1.0 + jnp.exp(-x))


def _silu(x):
    return x * _sigmoid(x)


def _mod_kernel(s_ref, w_ref, b_ref, o_ref):
    s = _silu(s_ref[...])
    o_ref[...] = _dot3(s, w_ref[...]) + b_ref[...]


def _mod_call(s_in, w_mod, b_mod):
    L, D, _ = w_mod.shape
    R = s_in.shape[0]
    return pl.pallas_call(
        _mod_kernel,
        grid=(L, 6),
        in_specs=[
            pl.BlockSpec((R, D), lambda l, j: (0, 0)),
            pl.BlockSpec((None, D, D), lambda l, j: (l, 0, j)),
            pl.BlockSpec((None, None, 1, D), lambda l, j: (l, j, 0, 0)),
        ],
        out_specs=pl.BlockSpec((None, None, R, D), lambda l, j: (l, j, 0, 0)),
        out_shape=jax.ShapeDtypeStruct((L, 6, R, D), F32),
        compiler_params=_cparams(("arbitrary", "arbitrary")),
        name="mod_table",
    )(s_in, w_mod, b_mod.reshape(L, 6, 1, D))


def _inproj_kernel(x_ref, sh_ref, sc_ref, g_ref, w_ref, o_ref, *, n_lat_tiles, tiles_per_batch, ctx_row):
    i = pl.program_id(0)
    b = jnp.where(i < n_lat_tiles, i // tiles_per_batch, ctx_row)
    x = x_ref[...]
    ms = jnp.mean(x * x, axis=-1, keepdims=True)
    h = x * lax.rsqrt(ms + NORM_EPS) * g_ref[...]
    h = h * (1.0 + sc_ref[pl.ds(b, 1), :]) + sh_ref[pl.ds(b, 1), :]
    hb = h.astype(BF16)
    for c in range(U_WIDTH // U_CHUNK):
        cs = slice(c * U_CHUNK, (c + 1) * U_CHUNK)
        o_ref[:, cs] = _dot(hb, w_ref[:, cs]).astype(BF16)


def _inproj_call(x, mods, l, norm_g, w_perm, n_lat_rows, rows_per_batch, ctx_row):
    N, D = x.shape
    R = mods.shape[2]
    tm = TM_IN
    kern = functools.partial(_inproj_kernel, n_lat_tiles=n_lat_rows // tm,
                             tiles_per_batch=rows_per_batch // tm, ctx_row=ctx_row)
    return pl.pallas_call(
        kern,
        grid=(N // tm,),
        in_specs=[
            pl.BlockSpec((tm, D), lambda i: (i, 0)),
            pl.BlockSpec((None, None, R, D), lambda i: (l, 0, 0, 0)),
            pl.BlockSpec((None, None, R, D), lambda i: (l, 1, 0, 0)),
            pl.BlockSpec((1, D), lambda i: (0, 0)),
            pl.BlockSpec((D, U_WIDTH), lambda i: (0, 0)),
        ],
        out_specs=pl.BlockSpec((tm, U_WIDTH), lambda i: (i, 0)),
        out_shape=jax.ShapeDtypeStruct((N, U_WIDTH), BF16),
        compiler_params=_cparams(("arbitrary",)),
        name="inproj",
    )(x, mods, mods, norm_g.reshape(1, D), w_perm)


def _gla_chunks(work):
    Lc = GLA_CHUNK
    r = lax.broadcasted_iota(jnp.int32, (Lc, Lc), 0)
    c = lax.broadcasted_iota(jnp.int32, (Lc, Lc), 1)
    lane_head = lax.broadcasted_iota(jnp.int32, (Lc, GLA_QK), 1) // GLA_DK
    vlane_head = lax.broadcasted_iota(jnp.int32, (Lc, GLA_V), 1) // GLA_DV
    t_i = lax.broadcasted_iota(jnp.int32, (Lc, GLA_HEADS * Lc), 0)
    s_i = lax.broadcasted_iota(jnp.int32, (Lc, GLA_HEADS * Lc), 1) % Lc
    row_head = lax.broadcasted_iota(jnp.int32, (GLA_V, GLA_QK), 0) // GLA_DV
    col_head = lax.broadcasted_iota(jnp.int32, (GLA_V, GLA_QK), 1) // GLA_DK

    zs = [_dot(gl, up) + bias for (_, _, gl, up, bias, _, _) in work]
    bs = []
    for z, (_, _, _, _, _, _, forward) in zip(zs, work):
        la = (jnp.minimum(z, 0.0) - jnp.log(1.0 + jnp.exp(-jnp.abs(z)))) * (1.0 / GLA_TAU)
        tri = jnp.where((r >= c) if forward else (r <= c), 1.0, 0.0).astype(BF16)
        la_hi, la_lo = _split_bf16(la)
        bs.append(_dot(tri, la_hi) + _dot(tri, la_lo))
    mids = []
    for b, (qk, v, _, _, _, _, forward) in zip(bs, work):
        q = qk[:, :GLA_QK].astype(F32) * (GLA_DK ** -0.5)
        k = qk[:, GLA_QK:].astype(F32)
        i_last, i_ref = (Lc - 1, Lc // 2) if forward else (0, Lc - 1 - Lc // 2)
        b_last = b[i_last:i_last + 1, :]
        b_ref = b[i_ref:i_ref + 1, :]
        qd = (q * jnp.exp(b - b_ref)).astype(BF16)
        kd = (k * jnp.exp(b_ref - b)).astype(BF16)
        qe = (q * jnp.exp(b)).astype(BF16)
        kl = (k * jnp.exp(b_last - b)).astype(BF16)
        a = jnp.exp(b_last)
        kd_blk = jnp.concatenate([jnp.where(lane_head == h, kd, jnp.zeros_like(kd)) for h in range(GLA_HEADS)], axis=0)
        v_blk = jnp.concatenate([jnp.where(vlane_head == h, v, jnp.zeros_like(v)) for h in range(GLA_HEADS)], axis=0)
        mids.append((qd, kd_blk, qe, kl, a, v_blk))
    scs = [_dot_nt(qd, kd_blk) for (qd, kd_blk, _, _, _, _) in mids]
    dss = [_dot_tn(v, kl) for (_, v, _, _, _, _, _), (_, _, _, kl, _, _) in zip(work, mids)]
    outs = []
    for sc, ds, (_, _, qe, _, a, v_blk), (_, _, _, _, _, st_ref, forward) in zip(scs, dss, mids, work):
        sc = jnp.where((s_i <= t_i) if forward else (s_i >= t_i), sc, 0.0).astype(BF16)
        st = st_ref[...]
        outs.append(_dot(sc, v_blk) + _dot_nt(qe, st.astype(BF16)))
        st_ref[...] = a * st + jnp.where(row_head == col_head, ds, 0.0)
    return outs


def _gla_kernel(qkf_ref, vf_ref, glf_ref, qkb_ref, vb_ref, glb_ref, up_ref, bias_ref, s0f_ref, s0b_ref,
                of_ref, ob_ref, sf_ref, sb_ref, stf, stb, *, nsub):
    n = pl.program_id(1)

    @pl.when(n == 0)
    def _():
        stf[...] = s0f_ref[...]
        stb[...] = s0b_ref[...]

    def body(j, carry):
        work, rows = [], []
        for u in range(GLA_CHUNKS_PER_BODY):
            jj = j * GLA_CHUNKS_PER_BODY + u
            rf = pl.ds(pl.multiple_of(jj * GLA_CHUNK, GLA_CHUNK), GLA_CHUNK)
            rb = pl.ds(pl.multiple_of((nsub - 1 - jj) * GLA_CHUNK, GLA_CHUNK), GLA_CHUNK)
            work += [
                (qkf_ref[rf, :], vf_ref[rf, :], glf_ref[rf, :], up_ref[:, :GLA_QK], bias_ref[:, :GLA_QK], stf, True),
                (qkb_ref[rb, :], vb_ref[rb, :], glb_ref[rb, :], up_ref[:, GLA_QK:], bias_ref[:, GLA_QK:], stb, False)]
            rows += [(of_ref, rf), (ob_ref, rb)]
        for (ref, rr), o in zip(rows, _gla_chunks(work)):
            ref[rr, :] = o
        return carry

    lax.fori_loop(0, nsub // GLA_CHUNKS_PER_BODY, body, 0)

    @pl.when(n == pl.num_programs(1) - 1)
    def _():
        sf_ref[...] = stf[...]
        sb_ref[...] = stb[...]


def _gla_call(u, row_off, B, T, s0f, s0b, up, bias):
    tb = min(TB_GLA, T)
    nb = T // tb
    off = row_off // tb
    fwd = lambda b, n: off + b * nb + n
    bwd = lambda b, n: off + b * nb + (nb - 1 - n)
    cqk, cv, cgl = OFF_GQK // 512, OFF_GV // 512, OFF_GL // 128
    st_spec = pl.BlockSpec((None, GLA_V, GLA_QK), lambda b, n: (b, 0, 0))
    st_shape = jax.ShapeDtypeStruct((B, GLA_V, GLA_QK), F32)
    o_shape = jax.ShapeDtypeStruct((B * T, GLA_V), F32)
    return pl.pallas_call(
        functools.partial(_gla_kernel, nsub=tb // GLA_CHUNK),
        grid=(B, nb),
        in_specs=[
            pl.BlockSpec((tb, 512), lambda b, n: (fwd(b, n), cqk)),
            pl.BlockSpec((tb, 512), lambda b, n: (fwd(b, n), cv)),
            pl.BlockSpec((tb, 128), lambda b, n: (fwd(b, n), cgl)),
            pl.BlockSpec((tb, 512), lambda b, n: (bwd(b, n), cqk)),
            pl.BlockSpec((tb, 512), lambda b, n: (bwd(b, n), cv)),
            pl.BlockSpec((tb, 128), lambda b, n: (bwd(b, n), cgl)),
            pl.BlockSpec((128, 2 * GLA_QK), lambda b, n: (0, 0)),
            pl.BlockSpec((1, 2 * GLA_QK), lambda b, n: (0, 0)),
            st_spec, st_spec,
        ],
        out_specs=[
            pl.BlockSpec((tb, GLA_V), lambda b, n: (b * nb + n, 0)),
            pl.BlockSpec((tb, GLA_V), lambda b, n: (b * nb + (nb - 1 - n), 0)),
            st_spec, st_spec,
        ],
        out_shape=[o_shape, o_shape, st_shape, st_shape],
        scratch_shapes=[pltpu.VMEM((GLA_V, GLA_QK), F32), pltpu.VMEM((GLA_V, GLA_QK), F32)],
        compiler_params=_cparams(("arbitrary", "arbitrary")),
        name="gla_scan",
    )(u, u, u, u, u, u, up, bias, s0f, s0b)


def _group_ms64(x):
    i = lax.broadcasted_iota(jnp.int32, (LANES, LANES), 0) // ATT_DH
    j = lax.broadcasted_iota(jnp.int32, (LANES, LANES), 1) // ATT_DH
    bd = jnp.where(i == j, 1.0, 0.0).astype(BF16)
    hi, lo = _split_bf16(x * x)
    return (_dot(hi, bd) + _dot(lo, bd)) * (1.0 / ATT_DH)


def _rope128(x, cos, s1, s2):
    return x * cos + pltpu.roll(x, LANES - ROPE_AXIS_DIM // 2, axis=1) * s1 + pltpu.roll(x, ROPE_AXIS_DIM // 2, axis=1) * s2


def _att_kernel(*refs, T, C, tk, use_rope):
    kx, vx, qh = refs[-3:]
    if T:
        (q_ref, kvl_ref, kvc_ref, cq_ref, s1q_ref, s2q_ref, ck_ref, s1k_ref, s2k_ref, qg_ref, kg_ref,
         o_ref) = refs[:-3]
    else:
        q_ref, kvc_ref, qg_ref, kg_ref, o_ref = refs[:-3]
    i = pl.program_id(1)
    lo_half = lax.broadcasted_iota(jnp.int32, (1, LANES), 1) < ATT_DH

    def put_kv(rows, kv, rope_tabs):
        k = kv[:, :LANES].astype(F32)
        v = kv[:, LANES:].astype(F32)
        k = k * lax.rsqrt(_group_ms64(k) + NORM_EPS) * kg_ref[...]
        if rope_tabs is not None:
            k = _rope128(k, *rope_tabs)
        k_sw = pltpu.roll(k, ATT_DH, axis=1)
        v_sw = pltpu.roll(v, ATT_DH, axis=1)
        kx[0, rows, :] = jnp.where(lo_half, k, k_sw).astype(BF16)
        kx[1, rows, :] = jnp.where(lo_half, k_sw, k).astype(BF16)
        rid = lax.broadcasted_iota(jnp.int32, (ATT_V_ROWS, 1), 0)
        for g, vg in enumerate((v, v_sw)):
            vt = vg.T[:ATT_V_ROWS, :]
            vx[g, :, rows] = jnp.where(rid < ATT_DH, vt, jnp.where(rid == ATT_DH, 1.0, 0.0)).astype(BF16)

    @pl.when(i == 0)
    def _():
        if T:
            def body(j, carry):
                rows = pl.ds(pl.multiple_of(j * tk, tk), tk)
                put_kv(rows, kvl_ref[rows, :], (ck_ref[rows, :], s1k_ref[rows, :], s2k_ref[rows, :]))
                return carry
            lax.fori_loop(0, T // tk, body, 0)
        put_kv(pl.ds(T, C), kvc_ref[...], None)

    n_slab = ATT_Q // LANES
    tq = q_ref.shape[0]
    for s in range(n_slab):
        q = q_ref[:, s * LANES:(s + 1) * LANES].astype(F32)
        q = q * lax.rsqrt(_group_ms64(q) + NORM_EPS) * qg_ref[...]
        if use_rope:
            q = _rope128(q, cq_ref[...], s1q_ref[...], s2q_ref[...])
        q = (q * (ATT_DH ** -0.5 * LOG2_E)).astype(BF16)
        qh[2 * s] = jnp.where(lo_half, q, jnp.zeros_like(q))
        qh[2 * s + 1] = jnp.where(lo_half, jnp.zeros_like(q), q)

    heads_per_kv = ATT_HEADS // ATT_KV_HEADS
    lo_rows = lax.broadcasted_iota(jnp.int32, (LANES, 1), 0) < ATT_DH
    for g in range(ATT_KV_HEADS):
        heads = range(g * heads_per_kv, (g + 1) * heads_per_kv)

        ahead = ATT_SCORE_LOOKAHEAD

        def chunk(rows, carry):
            kc = kx[g, rows, :]
            vt = vx[g, :, rows]
            scores = {h: _dot_nt(kc, qh[h]) for h in heads[:ahead]}
            new = []
            for n, (h, (m, acc)) in enumerate(zip(heads, carry)):
                if n + ahead < len(heads):
                    scores[heads[n + ahead]] = _dot_nt(kc, qh[heads[n + ahead]])
                sc = scores.pop(h)
                m_new = jnp.maximum(m, jnp.max(sc, axis=0, keepdims=True))
                p = jnp.exp2(sc - m_new).astype(BF16)
                acc = jnp.exp2(m - m_new) * acc + _dot(vt, p)
                new.append((m_new, acc))
            return tuple(new)

        init = (jnp.full((1, tq), -jnp.inf, F32), jnp.zeros((ATT_V_ROWS, tq), F32))
        carry = (init,) * heads_per_kv
        if T:
            carry = lax.fori_loop(
                0, T // tk, lambda j, cr: chunk(pl.ds(pl.multiple_of(j * tk, tk), tk), cr), carry)
        carry = chunk(pl.ds(T, C), carry)
        for hi in range(0, heads_per_kv, 2):
            s = (g * heads_per_kv + hi) // 2
            halves = [a[:ATT_DH, :] / a[ATT_DH:ATT_DH + 1, :] for _, a in (carry[hi], carry[hi + 1])]
            o_ref[:, s * LANES:(s + 1) * LANES] = jnp.concatenate(halves, axis=0).T.astype(BF16)


def _att_call(u, q_row_off, B, Tq, T, C, ctx_row_off, tabs, qg, kg):
    tq = min(TQ_ATT, Tq)
    nq = Tq // tq
    tk = TK_ATT
    S = T + C
    hpk = ATT_HEADS // ATT_KV_HEADS
    cq, ckv = OFF_AQ // 512, OFF_AKV // 256
    qoff = q_row_off // tq
    coff = ctx_row_off // C
    g_spec = pl.BlockSpec((1, LANES), lambda b, i: (0, 0))
    q_spec = pl.BlockSpec((tq, ATT_Q), lambda b, i: (qoff + b * nq + i, cq))
    kvc_spec = pl.BlockSpec((C, 2 * ATT_KV), lambda b, i: (coff + b, ckv))
    if T:
        tq_tab = pl.BlockSpec((tq, LANES), lambda b, i: (i, 0))
        tk_tab = pl.BlockSpec((T, LANES), lambda b, i: (0, 0))
        in_specs = [q_spec, pl.BlockSpec((T, 2 * ATT_KV), lambda b, i: (b, ckv)), kvc_spec,
                    tq_tab, tq_tab, tq_tab, tk_tab, tk_tab, tk_tab, g_spec, g_spec]
        args = (u, u, u, tabs[0], tabs[1], tabs[2], tabs[0], tabs[1], tabs[2], qg, kg)
    else:
        in_specs = [q_spec, kvc_spec, g_spec, g_spec]
        args = (u, u, qg, kg)
    return pl.pallas_call(
        functools.partial(_att_kernel, T=T, C=C, tk=tk, use_rope=bool(T)),
        grid=(B, nq),
        in_specs=in_specs,
        out_specs=pl.BlockSpec((tq, ATT_Q), lambda b, i: (b * nq + i, 0)),
        out_shape=jax.ShapeDtypeStruct((B * Tq, ATT_Q), BF16),
        scratch_shapes=[pltpu.VMEM((ATT_KV_HEADS, S, LANES), BF16), pltpu.VMEM((ATT_KV_HEADS, ATT_V_ROWS, S), BF16),
                        pltpu.VMEM((ATT_HEADS, tq, LANES), BF16)],
        compiler_params=pltpu.CompilerParams(dimension_semantics=("arbitrary", "arbitrary"),
                                             vmem_limit_bytes=VMEM_LIMIT_BYTES, flags=ATT_FLAGS),
        name="gqa_lat" if T else "gqa_ctx",
    )(*args)


def _merge_kernel(*refs, n_seq_tiles, fill_tail, **kw):
    xo_ref, h2_ref, lg_ref = refs[-3:]
    if not fill_tail:
        _merge_body(*refs, **kw)
        return
    i = pl.program_id(0)

    @pl.when(i < n_seq_tiles)
    def _():
        _merge_body(*refs, **kw)

    @pl.when(i >= n_seq_tiles)
    def _():
        for r in (xo_ref, h2_ref, lg_ref):
            r[...] = jnp.zeros_like(r)


def _merge_body(*refs, T, tm, tiles_per_batch, row_base):
    (of_ref, ob_ref, gr_ref, pu_ref, pp_ref, pn_ref, gt_ref, ya_ref, x_ref, g1_ref, sh2_ref, sc2_ref,
     gng_ref, band_ref, pw_ref, ps_ref, wb_ref, wo_ref, n2g_ref, wrh_ref, wrl_ref, br_ref) = refs[:22]
    xo_ref, h2_ref, lg_ref = refs[-3:]
    i = pl.program_id(0)
    it = i % tiles_per_batch
    b = i // tiles_per_batch if row_base is None else row_base

    gt = gt_ref[...].astype(F32)
    z = _sigmoid(gt[:, D_MODEL:2 * D_MODEL]) * _dot(ya_ref[...], wb_ref[1])

    prev = jnp.where(it > 0, pp_ref[...], jnp.zeros_like(pp_ref[...]))
    nxt = jnp.where(it < tiles_per_batch - 1, pn_ref[...], jnp.zeros_like(pn_ref[...]))
    ext = jnp.concatenate([prev, pu_ref[...], nxt], axis=0)
    sub = 128
    n_sub = tm // sub
    ext_sub = [ext[r * sub:r * sub + sub + 2 * POOL_HALO, :] for r in range(n_sub)]
    groups = [slice(gi * POOL_GROUP, (gi + 1) * POOL_GROUP) for gi in range(len(POOL_WINDOWS))]
    wsums = [[_dot(band_ref[gi], e[:, gs]) for gi, gs in enumerate(groups)] for e in ext_sub]
    yp_rows = []
    for r, e in enumerate(ext_sub):
        t = it * tm + r * sub + lax.broadcasted_iota(jnp.int32, (sub, 1), 0)
        cols = []
        for gi, win in enumerate(POOL_WINDOWS):
            cnt = (jnp.minimum(t + win // 2, T) - jnp.maximum(t - win // 2, 0)).astype(F32)
            d = wsums[r][gi] / cnt - e[POOL_HALO:POOL_HALO + sub, groups[gi]].astype(F32)
            cols.append(_dot(d.astype(BF16), pw_ref[gi]))
        yp_rows.append(jnp.concatenate(cols, axis=1))
    y_pool = (jnp.concatenate(yp_rows, axis=0) * ps_ref[...]).astype(BF16)
    z = z + _sigmoid(gt[:, 2 * D_MODEL:]) * _dot(y_pool, wb_ref[2])

    o = of_ref[...] + ob_ref[...]
    gr = gr_ref[...].astype(F32)
    parts = []
    for h in range(GLA_HEADS):
        hs = slice(h * GLA_DV, (h + 1) * GLA_DV)
        oh = o[:, hs]
        ms = jnp.mean(oh * oh, axis=-1, keepdims=True)
        parts.append(oh * lax.rsqrt(ms + NORM_EPS) * gng_ref[...])
    y_gla = (jnp.concatenate(parts, axis=1) * _silu(gr)).astype(BF16)
    z = z + _sigmoid(gt[:, :D_MODEL]) * _dot(y_gla, wb_ref[0])
    y = _dot(z.astype(BF16), wo_ref[...])
    xn = x_ref[...] + g1_ref[pl.ds(b, 1), :] * y
    xo_ref[...] = xn
    ms = jnp.mean(xn * xn, axis=-1, keepdims=True)
    h2 = xn * lax.rsqrt(ms + NORM_EPS) * n2g_ref[...]
    h2 = h2 * (1.0 + sc2_ref[pl.ds(b, 1), :]) + sh2_ref[pl.ds(b, 1), :]
    _store_token_rows(h2_ref, h2)
    hh, hl = _split_bf16(h2)
    lg_ref[...] = _dot(hh, wrh_ref[...]) + _dot(hh, wrl_ref[...]) + _dot(hl, wrh_ref[...]) + br_ref[...]


def _merge_call(o_f, o_b, u, y_att, xin, x_row_off, B, T, mods, l, mod_row, wts, n_out_rows, prev_outs):
    tm = min(TM_MERGE, T)
    tpb = T // tm
    D = D_MODEL
    R = mods.shape[2]
    ro = x_row_off // tm
    ro16 = x_row_off // POOL_HALO
    r16 = tm // POOL_HALO
    n16 = u.shape[0] // POOL_HALO
    cgr, cpu = OFF_GR // 512, OFF_PU // 512
    n_seq_tiles = B * tpb
    n_grid = n_seq_tiles if prev_outs is not None else (n_out_rows - x_row_off) // tm
    fill_tail = n_grid > n_seq_tiles
    ic = lambda i: jnp.minimum(i, n_seq_tiles - 1)
    full = lambda shp: pl.BlockSpec(shp, lambda i: (0,) * len(shp))
    mod_spec = lambda k: pl.BlockSpec((None, None, R, D), lambda i: (l, k, 0, 0))
    in_specs = [
        pl.BlockSpec((tm, GLA_V), lambda i: (ic(i), 0)),
        pl.BlockSpec((tm, GLA_V), lambda i: (ic(i), 0)),
        pl.BlockSpec((tm, 512), lambda i: (ro + ic(i), cgr)),
        pl.BlockSpec((tm, 512), lambda i: (ro + ic(i), cpu)),
        pl.BlockSpec((POOL_HALO, 512), lambda i: (jnp.maximum(ro16 + ic(i) * r16 - 1, 0), cpu)),
        pl.BlockSpec((POOL_HALO, 512), lambda i: (jnp.minimum(ro16 + (ic(i) + 1) * r16, n16 - 1), cpu)),
        pl.BlockSpec((tm, N_BRANCH * D), lambda i: (ro + ic(i), 0)),
        pl.BlockSpec((tm, ATT_Q), lambda i: (ic(i), 0)),
        pl.BlockSpec((tm, D), lambda i: (ro + ic(i), 0)),
        mod_spec(2), mod_spec(3), mod_spec(4),
        full((1, GLA_DV)), full((4, 128, 128 + 2 * POOL_HALO)), full((4, POOL_GROUP, POOL_GROUP)), full((1, POOL_WIDTH)),
        full((N_BRANCH, BRANCH_WIDTH, D)), full((D, D)), full((1, D)), full((D, ROUTER_W)), full((D, ROUTER_W)),
        full((1, ROUTER_W)),
    ]
    args = [o_f, o_b, u, u, u, u, u, y_att, xin, mods, mods, mods, *wts]
    out_shape = [jax.ShapeDtypeStruct((n_out_rows, D), F32), jax.ShapeDtypeStruct((n_out_rows * TOK_ROWS, LANES), F32),
                 jax.ShapeDtypeStruct((n_out_rows, ROUTER_W), F32)]
    out_specs = [pl.BlockSpec((tm, D), lambda i: (ro + i, 0)), pl.BlockSpec((tm * TOK_ROWS, LANES), lambda i: (ro + i, 0)),
                 pl.BlockSpec((tm, ROUTER_W), lambda i: (ro + i, 0))]
    aliases = {}
    if prev_outs is not None:
        n_in = len(args)
        in_specs += [pl.BlockSpec(memory_space=pl.ANY)] * 3
        args += list(prev_outs)
        aliases = {n_in: 0, n_in + 1: 1, n_in + 2: 2}
    kern = functools.partial(_merge_kernel, n_seq_tiles=n_seq_tiles, fill_tail=fill_tail,
                             T=T, tm=tm, tiles_per_batch=tpb, row_base=mod_row)
    return pl.pallas_call(
        kern,
        grid=(n_grid,),
        in_specs=in_specs,
        out_specs=out_specs,
        out_shape=out_shape,
        input_output_aliases=aliases,
        compiler_params=_cparams(("arbitrary",)),
        name="merge",
    )(*args)


def _router_kernel(lg_ref, out_ref, cnt_ref, carry, pstart, *, tm, bm):
    ph = pl.program_id(0)
    i = pl.program_id(1)
    row = lax.broadcasted_iota(jnp.int32, (LANES, 1), 0)
    neg = -jnp.inf
    x = lg_ref[...].T
    colmax = lambda a: jnp.max(a, axis=0, keepdims=True)
    colmin = lambda a: jnp.min(a, axis=0, keepdims=True)
    colsum = lambda a: jnp.sum(a, axis=0, keepdims=True)

    @pl.when((ph == 0) & (i == 0))
    def _():
        carry[...] = jnp.zeros_like(carry)

    gl = jnp.where(row < N_GROUPS, x, neg)
    gmax = colmax(gl)
    gsum = colsum(jnp.exp(gl - gmax))
    grp = colmin(jnp.where(gl == gmax, row, LANES))
    e_row = row - N_GROUPS
    row_grp = sum((e_row >= EXP_PER_GROUP * k).astype(jnp.int32) for k in range(1, N_GROUPS))
    row_grp = jnp.where(e_row < 0, -1, jnp.where(e_row < N_EXPERTS, row_grp, -1))
    el = jnp.where(row_grp == grp, x, neg)
    emax = colmax(el)
    esum = colsum(jnp.exp(el - emax))
    i1 = colmin(jnp.where(el == emax, row, LANES))
    el2 = jnp.where(row == i1, neg, el)
    m2 = colmax(el2)
    i2 = colmin(jnp.where(el2 == m2, row, LANES))
    p1 = 1.0 / esum
    p2 = jnp.exp(m2 - emax) / esum
    pg = 1.0 / gsum
    w1 = pg * p1 / (p1 + p2)
    w2 = pg * p2 / (p1 + p2)
    oh1 = row == i1
    oh2 = row == i2
    oh = jnp.where(oh1, 1.0, jnp.where(oh2, 1.0, 0.0))

    @pl.when(ph == 0)
    def _():
        carry[...] += jnp.sum(oh, axis=1, keepdims=True)

    @pl.when((ph == 1) & (i == 0))
    def _():
        cnt = carry[...]
        cnt_ref[...] = jnp.broadcast_to(cnt, cnt_ref.shape)
        nb = jnp.floor((cnt + (bm - 1)) * (1.0 / bm))
        hi = jnp.floor(nb * (1.0 / 16.0))
        lo = nb - 16.0 * hi
        r = lax.broadcasted_iota(jnp.int32, (LANES, LANES), 0)
        c = lax.broadcasted_iota(jnp.int32, (LANES, LANES), 1)
        lower = jnp.where(c < r, 1.0, 0.0).astype(BF16)
        hib = jnp.broadcast_to(hi, (LANES, LANES)).astype(BF16)
        lob = jnp.broadcast_to(lo, (LANES, LANES)).astype(BF16)
        pre = 16.0 * _dot(lower, hib) + _dot(lower, lob)
        pstart[...] = pre[:, 0:1] * bm
        carry[...] = jnp.zeros_like(carry)

    @pl.when(ph == 1)
    def _():
        s_i = lax.broadcasted_iota(jnp.int32, (tm, tm), 0)
        t_i = lax.broadcasted_iota(jnp.int32, (tm, tm), 1)
        upper = jnp.where(s_i < t_i, 1.0, 0.0).astype(BF16)
        before = _dot(oh.astype(BF16), upper)
        slot = pstart[...] + carry[...] + before
        d1 = colsum(jnp.where(oh1, slot, 0.0))
        d2 = colsum(jnp.where(oh2, slot, 0.0))
        carry[...] += jnp.sum(oh, axis=1, keepdims=True)
        r8 = lax.broadcasted_iota(jnp.int32, (8, 1), 0)
        top = jnp.where(r8 == 0, d1, jnp.where(r8 == 1, d2, jnp.where(r8 == 2, w1, jnp.where(r8 == 3, w2, 0.0))))
        out_ref[...] = jnp.concatenate([top, jnp.zeros((LANES - 8, tm), F32)], axis=0).T


def _router_call(logits, bm):
    N = logits.shape[0]
    tm = TM_ROUTE
    return pl.pallas_call(
        functools.partial(_router_kernel, tm=tm, bm=bm),
        grid=(2, N // tm),
        in_specs=[pl.BlockSpec((tm, ROUTER_W), lambda p, i: (i, 0))],
        out_specs=[pl.BlockSpec((tm, LANES), lambda p, i: (i * p, 0)), pl.BlockSpec((LANES, LANES), lambda p, i: (0, 0))],
        out_shape=[jax.ShapeDtypeStruct((N, LANES), F32), jax.ShapeDtypeStruct((LANES, LANES), F32)],
        scratch_shapes=[pltpu.VMEM((LANES, 1), F32), pltpu.VMEM((LANES, 1), F32)],
        compiler_params=_cparams(("arbitrary", "arbitrary")),
        name="router",
    )(logits)


def _by_parity(i, fn):
    for p in range(2):
        pl.when(i % 2 == p)(functools.partial(fn, p))


def _dispatch_kernel(idx_hbm, h2_ref, xs_in, xs_hbm, idx_a, idx_b, isem, dsem, *, tm, n_tiles):
    del xs_in
    i = pl.program_id(0)
    idx_s = (idx_a, idx_b)

    def idx_copy(j, slot):
        return pltpu.make_async_copy(idx_hbm.at[j], idx_s[slot], isem.at[slot])

    def row_copy(r, slot):
        return pltpu.make_async_copy(h2_ref.at[r], xs_hbm.at[slot], dsem)

    @pl.when(i == 0)
    def _():
        idx_copy(0, 0).start()

    def step(p):
        @pl.when(i + 1 < n_tiles)
        def _():
            idx_copy(i + 1, 1 - p).start()

        idx_copy(i, p).wait()

        def issue(r0, c):
            for u in range(DMA_UNROLL):
                r = r0 * DMA_UNROLL + u
                for k in range(TOP_K):
                    row_copy(r, idx_s[p][0, TOP_K * r + k]).start()
            return c

        lax.fori_loop(0, tm // DMA_UNROLL, issue, 0)

    _by_parity(i, step)

    def drain(r0, c):
        for _ in range(TOP_K * DMA_UNROLL):
            row_copy(0, 0).wait()
        return c

    lax.fori_loop(0, tm // DMA_UNROLL, drain, 0)


def _dispatch_call(idx, h2, n_slots):
    n_tiles = idx.shape[0]
    tm = TM_DISP
    xs0 = jnp.zeros((n_slots, TOK_ROWS, LANES), F32)
    return pl.pallas_call(
        functools.partial(_dispatch_kernel, tm=tm, n_tiles=n_tiles),
        grid=(n_tiles,),
        in_specs=[pl.BlockSpec(memory_space=pl.ANY),
                  pl.BlockSpec((tm, TOK_ROWS, LANES), lambda i: (i, 0, 0)),
                  pl.BlockSpec(memory_space=pl.ANY)],
        out_specs=pl.BlockSpec(memory_space=pl.ANY),
        out_shape=jax.ShapeDtypeStruct((n_slots, TOK_ROWS, LANES), F32),
        input_output_aliases={2: 0},
        scratch_shapes=[pltpu.SMEM((1, TOP_K * tm), jnp.int32), pltpu.SMEM((1, TOP_K * tm), jnp.int32),
                        pltpu.SemaphoreType.DMA((2,)),
                        pltpu.SemaphoreType.DMA(())],
        compiler_params=pltpu.CompilerParams(dimension_semantics=("arbitrary",), vmem_limit_bytes=VMEM_LIMIT_BYTES,
                                             has_side_effects=True),
        name="dispatch",
    )(idx, h2, xs0)


def _expert_kernel(be_ref, nu_ref, xs_ref, wg_ref, wu_ref, wd_ref, ys_ref, wgb, wub, wdb, *, bm):
    i = pl.program_id(0)

    @pl.when((i == 0) | (be_ref[i] != be_ref[jnp.maximum(i - 1, 0)]))
    def _():
        wgb[...] = wg_ref[...].astype(BF16)
        wub[...] = wu_ref[...].astype(BF16)
        wdb[...] = wd_ref[...].astype(BF16)

    @pl.when(i < nu_ref[0])
    def _():
        xb = _load_token_rows(xs_ref, bm).astype(BF16)
        g = _dot(xb, wgb[...])
        up = _dot(xb, wub[...])
        hmid = (_silu(g) * up).astype(BF16)
        _store_token_rows(ys_ref, _dot(hmid, wdb[...]))

    @pl.when(i >= nu_ref[0])
    def _():
        ys_ref[...] = jnp.zeros_like(ys_ref)


def _expert_call(blk_expert, nused, xs, wg, wu, wd, l):
    bm = BM_MOE
    D = D_MODEL
    nblk = blk_expert.shape[0]
    w_in_spec = pl.BlockSpec((None, None, D, D_EXPERT), lambda i, be, nu: (l, be[i], 0, 0))
    grid_spec = pltpu.PrefetchScalarGridSpec(
        num_scalar_prefetch=2,
        grid=(nblk,),
        in_specs=[
            pl.BlockSpec((bm * TOK_ROWS, LANES), lambda i, be, nu: (jnp.minimum(i, nu[0] - 1), 0)),
            w_in_spec, w_in_spec,
            pl.BlockSpec((None, None, D_EXPERT, D), lambda i, be, nu: (l, be[i], 0, 0)),
        ],
        out_specs=pl.BlockSpec((bm * TOK_ROWS, LANES), lambda i, be, nu: (i, 0)),
        scratch_shapes=[pltpu.VMEM((D, D_EXPERT), BF16), pltpu.VMEM((D, D_EXPERT), BF16),
                        pltpu.VMEM((D_EXPERT, D), BF16)],
    )
    return pl.pallas_call(
        functools.partial(_expert_kernel, bm=bm),
        grid_spec=grid_spec,
        out_shape=jax.ShapeDtypeStruct(xs.shape, F32),
        compiler_params=_cparams(("arbitrary",)),
        name="experts",
    )(blk_expert, nused, xs, wg, wu, wd)


def _combine_kernel(idx_hbm, ys_hbm, x_ref, r_ref, g2_ref, fg_ref, o_ref, idx_a, idx_b, isem, ybuf, gsem, *,
                    tm, n_tiles, n_lat_tiles, tiles_per_batch, ctx_row, final):
    i = pl.program_id(0)
    idx_s = (idx_a, idx_b)

    def idx_copy(j, slot):
        return pltpu.make_async_copy(idx_hbm.at[j], idx_s[slot], isem.at[slot])

    def row_copy(slot, r, k, src):
        rows = pl.ds(pl.multiple_of((k * tm + r) * TOK_ROWS, TOK_ROWS), TOK_ROWS)
        return pltpu.make_async_copy(ys_hbm.at[src], ybuf.at[slot, rows, :], gsem.at[slot])

    def start_gather(slot):
        def issue(r0, c):
            for u in range(DMA_UNROLL):
                r = r0 * DMA_UNROLL + u
                for k in range(TOP_K):
                    row_copy(slot, r, k, idx_s[slot][0, TOP_K * r + k]).start()
            return c
        lax.fori_loop(0, tm // DMA_UNROLL, issue, 0)

    def wait_gather(slot):
        def drain(r0, c):
            for _ in range(TOP_K * DMA_UNROLL):
                row_copy(slot, 0, 0, 0).wait()
            return c
        lax.fori_loop(0, tm // DMA_UNROLL, drain, 0)

    @pl.when(i == 0)
    def _():
        idx_copy(0, 0).start()
        idx_copy(0, 0).wait()
        start_gather(0)
        if n_tiles > 1:
            idx_copy(1, 1).start()

    def step(p):
        @pl.when(i + 1 < n_tiles)
        def _():
            idx_copy(i + 1, 1 - p).wait()
            start_gather(1 - p)

        @pl.when(i + 2 < n_tiles)
        def _():
            idx_copy(i + 2, p).start()

        wait_gather(p)
        yb = ybuf.at[p]
        y0 = _load_token_rows(yb, tm, 0)
        y1 = _load_token_rows(yb, tm, tm * TOK_ROWS)
        m = r_ref[:, 2:3] * y0 + r_ref[:, 3:4] * y1
        b = jnp.where(i < n_lat_tiles, i // tiles_per_batch, ctx_row)
        xn = x_ref[...] + g2_ref[pl.ds(b, 1), :] * m
        if final:
            ms = jnp.mean(xn * xn, axis=-1, keepdims=True)
            xn = xn * lax.rsqrt(ms + NORM_EPS) * fg_ref[...]
        o_ref[...] = xn

    _by_parity(i, step)


def _combine_call(idx, ys, x_mid, route, mods, l, final_g, n_lat_rows, rows_per_batch, ctx_row, final):
    tm = TM_COMB
    D = D_MODEL
    R = mods.shape[2]
    n_rows = x_mid.shape[0]
    n_tiles = n_rows // tm
    kern = functools.partial(_combine_kernel, tm=tm, n_tiles=n_tiles, n_lat_tiles=n_lat_rows // tm,
                             tiles_per_batch=rows_per_batch // tm, ctx_row=ctx_row, final=final)
    return pl.pallas_call(
        kern,
        grid=(n_tiles,),
        in_specs=[
            pl.BlockSpec(memory_space=pl.ANY),
            pl.BlockSpec(memory_space=pl.ANY),
            pl.BlockSpec((tm, D), lambda i: (i, 0)),
            pl.BlockSpec((tm, LANES), lambda i: (i, 0)),
            pl.BlockSpec((None, None, R, D), lambda i: (l, 5, 0, 0)),
            pl.BlockSpec((1, D), lambda i: (0, 0)),
        ],
        out_specs=pl.BlockSpec((tm, D), lambda i: (i, 0)),
        out_shape=jax.ShapeDtypeStruct((n_rows, D), F32),
        scratch_shapes=[pltpu.SMEM((1, TOP_K * tm), jnp.int32), pltpu.SMEM((1, TOP_K * tm), jnp.int32),
                        pltpu.SemaphoreType.DMA((2,)),
                        pltpu.VMEM((2, TOP_K * tm * TOK_ROWS, LANES), F32), pltpu.SemaphoreType.DMA((2,))],
        compiler_params=_cparams(("arbitrary",)),
        name="combine",
    )(idx, ys, x_mid, route, mods, final_g.reshape(1, D))


def _permute_w_in(w_in):
    gq, gk, gv, gr, glf, glb, aq, ak, av, pu, gt = jnp.split(w_in, np.cumsum(IN_SIZES)[:-1].tolist(), axis=-1)
    pad = jnp.zeros(w_in.shape[:-1] + (U_WIDTH - sum(IN_SIZES),), w_in.dtype)
    return jnp.concatenate([gt, gq, gk, gv, gr, aq, pu, ak, av, glf, glb, pad], axis=-1).astype(BF16)


def _rope_tables(T):
    rows = T // GRID_W
    row = np.repeat(np.arange(rows), GRID_W).astype(np.float32)
    col = np.tile(np.arange(GRID_W), rows).astype(np.float32)
    inv = jnp.asarray(ROPE_THETA, F32) ** (-jnp.arange(0, ROPE_AXIS_DIM, 2, dtype=F32) / ROPE_AXIS_DIM)
    ang_r = jnp.asarray(row)[:, None] * inv
    ang_c = jnp.asarray(col)[:, None] * inv
    zero = jnp.zeros_like(ang_r)
    cos = jnp.concatenate([jnp.cos(ang_r)] * 2 + [jnp.cos(ang_c)] * 2, axis=1)
    s1 = jnp.concatenate([-jnp.sin(ang_r), zero, -jnp.sin(ang_c), zero], axis=1)
    s2 = jnp.concatenate([zero, jnp.sin(ang_r), zero, jnp.sin(ang_c)], axis=1)
    return tuple(jnp.concatenate([t, t], axis=1) for t in (cos, s1, s2))


def _pool_bands():
    i = np.arange(128)[:, None]
    j = np.arange(128 + 2 * POOL_HALO)[None, :]
    bands = [((j >= i + POOL_HALO - w // 2) & (j < i + POOL_HALO + w // 2)).astype(np.float32) for w in POOL_WINDOWS]
    return jnp.asarray(np.stack(bands), BF16)


def _block_table(counts, n_tok, bm):
    cnt = counts[N_GROUPS:N_GROUPS + N_EXPERTS, 0].astype(jnp.int32)
    pad_end = jnp.cumsum((cnt + bm - 1) // bm * bm)
    nblk = -(-(n_tok * TOP_K + N_EXPERTS * (bm - 1)) // bm)
    blk_start = jnp.arange(nblk, dtype=jnp.int32) * bm
    blk_expert = jnp.minimum(jnp.sum((pad_end[None, :] <= blk_start[:, None]).astype(jnp.int32), axis=1), N_EXPERTS - 1)
    nused = (pad_end[-1] // bm).astype(jnp.int32).reshape(1)
    return blk_expert.astype(jnp.int32), nused, nblk


def _slot_tiles(route, tm):
    n = route.shape[0]
    return route[:, :TOP_K].astype(jnp.int32).reshape(n // tm, 1, TOP_K * tm)


def kernel(x, c, ctx, c_ctx, w_mod, b_mod, norm1_g, norm2_g, w_in, gla_a_up_f, gla_a_bias_f, gla_a_up_b, gla_a_bias_b, gla_norm_g, att_qn_g, att_kn_g, pool_w, pool_scale, w_branch, w_out, moe_w_group, moe_b_group, moe_w_expert, moe_b_expert, moe_w_gate, moe_w_up, moe_w_down, final_g):
    B, T, D = x.shape
    C = ctx.shape[1]
    L = w_mod.shape[0]
    n_lat, n_ctx = B * T, B * C
    MOD_ROWS = 16
    assert D == D_MODEL and B < MOD_ROWS and T % TQ_ATT == 0 and C % GLA_CHUNK == 0

    s_in = jnp.concatenate([c, c_ctx[None], jnp.zeros((MOD_ROWS - B - 1, D), F32)], axis=0)
    mods = _mod_call(s_in, w_mod, b_mod)
    w_perm = _permute_w_in(w_in)
    tabs = _rope_tables(T)
    bands = _pool_bands()
    zero_state = jnp.zeros((B, GLA_V, GLA_QK), F32)

    xall = jnp.concatenate([x.reshape(n_lat, D), ctx.reshape(n_ctx, D)], axis=0)
    out = None
    for l in range(L):
        want_ctx = l < L - 1
        n_rows = xall.shape[0]
        u = _inproj_call(xall, mods, l, norm1_g[l], w_perm[l], n_lat, T, B)

        up = jnp.zeros((128, 2 * GLA_QK), F32)
        up = up.at[:GLA_RANK, :GLA_QK].set(gla_a_up_f[l]).at[GLA_RANK:2 * GLA_RANK, GLA_QK:].set(gla_a_up_b[l]).astype(BF16)
        bias = jnp.concatenate([gla_a_bias_f[l], gla_a_bias_b[l]]).reshape(1, 2 * GLA_QK)
        ofc, obc, sfc, sbc = _gla_call(u, n_lat, B, C, zero_state, zero_state, up, bias)
        of, ob, _, _ = _gla_call(u, 0, B, T, sfc, sbc, up, bias)

        qg = jnp.tile(att_qn_g[l], 2).reshape(1, LANES)
        kg = jnp.tile(att_kn_g[l], 2).reshape(1, LANES)
        ya = _att_call(u, 0, B, T, T, C, n_lat, tabs, qg, kg)

        wr = jnp.zeros((D, ROUTER_W), F32).at[:, :N_GROUPS].set(moe_w_group[l]).at[:, N_GROUPS:N_GROUPS + N_EXPERTS].set(moe_w_expert[l])
        wrh, wrl = _split_bf16(wr)
        br = jnp.zeros((1, ROUTER_W), F32).at[0, :N_GROUPS].set(moe_b_group[l]).at[0, N_GROUPS:N_GROUPS + N_EXPERTS].set(moe_b_expert[l])
        wts = (gla_norm_g[l].reshape(1, GLA_DV), bands, pool_w[l].astype(BF16), pool_scale[l].reshape(1, POOL_WIDTH),
               w_branch[l].astype(BF16), w_out[l].astype(BF16), norm2_g[l].reshape(1, D), wrh, wrl, br)
        n_tok = n_rows if want_ctx else n_lat
        outs = _merge_call(of, ob, u, ya, xall, 0, B, T, mods, l, None, wts, n_tok, None)
        if want_ctx:
            yac = _att_call(u, n_lat, B, C, 0, C, n_lat, None, qg, kg)
            outs = _merge_call(ofc, obc, u, yac, xall, n_lat, B, C, mods, l, B, wts, n_tok, outs)
        x_mid, h2, logits = outs

        route, counts = _router_call(logits, BM_MOE)
        blk_expert, nused, nblk = _block_table(counts, n_tok, BM_MOE)
        n_slots = nblk * BM_MOE
        xs = _dispatch_call(_slot_tiles(route, TM_DISP), h2.reshape(n_tok, TOK_ROWS, LANES), n_slots)
        ys = _expert_call(blk_expert, nused, xs.reshape(n_slots * TOK_ROWS, LANES), moe_w_gate, moe_w_up, moe_w_down, l)
        xall = _combine_call(_slot_tiles(route, TM_COMB), ys.reshape(n_slots, TOK_ROWS, LANES), x_mid, route, mods, l,
                             final_g, n_lat, T, B, final=not want_ctx)
    return xall[:n_lat].reshape(B, T, D)
```

```python
import functools

import numpy as np
import jax
import jax.numpy as jnp
from jax import lax
from jax.experimental import pallas as pl
from jax.experimental.pallas import tpu as pltpu

F32 = jnp.float32
BF16 = jnp.bfloat16

VMEM_LIMIT_BYTES = 56 * 1024 * 1024
LANES = 128

D_MODEL = 1024
GRID_W = 64
NORM_EPS = 1e-6
GLA_HEADS, GLA_DK, GLA_DV, GLA_RANK, GLA_TAU, GLA_CHUNK = 4, 64, 128, 16, 16.0, 64
GLA_QK, GLA_V = GLA_HEADS * GLA_DK, GLA_HEADS * GLA_DV
ATT_HEADS, ATT_KV_HEADS, ATT_DH = 8, 2, 64
ROPE_THETA, ROPE_AXIS_DIM = 10000.0, 32
ATT_Q, ATT_KV = ATT_HEADS * ATT_DH, ATT_KV_HEADS * ATT_DH
POOL_WINDOWS, POOL_GROUP = (2, 4, 8, 16), 128
POOL_WIDTH = POOL_GROUP * len(POOL_WINDOWS)
POOL_HALO = 16
N_BRANCH, BRANCH_WIDTH = 3, 512
N_GROUPS, EXP_PER_GROUP, TOP_K, D_EXPERT = 4, 8, 2, 512
N_EXPERTS = N_GROUPS * EXP_PER_GROUP
IN_SIZES = (GLA_QK, GLA_QK, GLA_V, GLA_V, GLA_RANK, GLA_RANK, ATT_Q, ATT_KV, ATT_KV, POOL_WIDTH, N_BRANCH * D_MODEL)

OFF_GT, OFF_GQK, OFF_GV, OFF_GR, OFF_AQ, OFF_PU, OFF_AKV, OFF_GL = 0, 3072, 3584, 4096, 4608, 5120, 5632, 5888
U_WIDTH = 6144
U_CHUNK = 512

TM_IN = 512
TB_GLA = 256
GLA_CHUNKS_PER_BODY = 4
TQ_ATT = 1024
TK_ATT = 512
ATT_SUB_ROWS = 64
LOG2_E = 1.4426950408889634
ATT_FLAGS = {}
ATT_V_ROWS = 80
ATT_SCORE_LOOKAHEAD = 2
TM_MERGE = 512
BM_MOE = 512
EXPERT_PARTS = 2
TM_ROUTE = 512
TM_DISP = 256
TM_COMB = 256
ROUTER_W = 128
TOK_ROWS = D_MODEL // LANES
DMA_UNROLL = 8


def _cparams(sem):
    return pltpu.CompilerParams(dimension_semantics=sem, vmem_limit_bytes=VMEM_LIMIT_BYTES)


def _split_bf16(a):
    hi = a.astype(BF16)
    lo = (a - hi.astype(F32)).astype(BF16)
    return hi, lo


def _dot(a, b):
    return jnp.dot(a, b, preferred_element_type=F32)


def _dot_nt(a, b):
    return lax.dot_general(a, b, (((1,), (1,)), ((), ())), preferred_element_type=F32)


def _dot_tn(a, b):
    return lax.dot_general(a, b, (((0,), (0,)), ((), ())), preferred_element_type=F32)


def _dot3(a, b):
    ah, al = _split_bf16(a)
    bh, bl = _split_bf16(b)
    return _dot(ah, bh) + _dot(ah, bl) + _dot(al, bh)


def _load_token_rows(ref, n_tok, row0=0):
    return jnp.concatenate([ref[pl.ds(row0 + s, n_tok, stride=TOK_ROWS), :] for s in range(TOK_ROWS)], axis=1)


def _store_token_rows(ref, val, row0=0):
    n_tok = val.shape[0]
    for s in range(TOK_ROWS):
        ref[pl.ds(row0 + s, n_tok, stride=TOK_ROWS), :] = val[:, s * LANES:(s + 1) * LANES]


def _sigmoid(x):
    return 1.0 / (1.0 + jnp.exp(-x))


def _silu(x):
    return x * _sigmoid(x)


def _mod_kernel(s_ref, w_ref, b_ref, o_ref):
    s = _silu(s_ref[...])
    o_ref[...] = _dot3(s, w_ref[...]) + b_ref[...]


def _mod_call(s_in, w_mod, b_mod):
    L, D, _ = w_mod.shape
    R = s_in.shape[0]
    return pl.pallas_call(
        _mod_kernel,
        grid=(L, 6),
        in_specs=[
            pl.BlockSpec((R, D), lambda l, j: (0, 0)),
            pl.BlockSpec((None, D, D), lambda l, j: (l, 0, j)),
            pl.BlockSpec((None, None, 1, D), lambda l, j: (l, j, 0, 0)),
        ],
        out_specs=pl.BlockSpec((None, None, R, D), lambda l, j: (l, j, 0, 0)),
        out_shape=jax.ShapeDtypeStruct((L, 6, R, D), F32),
        compiler_params=_cparams(("arbitrary", "arbitrary")),
        name="mod_table",
    )(s_in, w_mod, b_mod.reshape(L, 6, 1, D))


def _inproj_kernel(x_ref, sh_ref, sc_ref, g_ref, w_ref, o_ref, *, n_lat_tiles, tiles_per_batch, ctx_row):
    i = pl.program_id(0)
    b = jnp.where(i < n_lat_tiles, i // tiles_per_batch, ctx_row)
    x = x_ref[...]
    ms = jnp.mean(x * x, axis=-1, keepdims=True)
    h = x * lax.rsqrt(ms + NORM_EPS) * g_ref[...]
    h = h * (1.0 + sc_ref[pl.ds(b, 1), :]) + sh_ref[pl.ds(b, 1), :]
    hb = h.astype(BF16)
    for c in range(U_WIDTH // U_CHUNK):
        cs = slice(c * U_CHUNK, (c + 1) * U_CHUNK)
        o_ref[:, cs] = _dot(hb, w_ref[:, cs]).astype(BF16)


def _inproj_call(x, mods, l, norm_g, w_perm, n_lat_rows, rows_per_batch, ctx_row):
    N, D = x.shape
    R = mods.shape[2]
    tm = TM_IN
    kern = functools.partial(_inproj_kernel, n_lat_tiles=n_lat_rows // tm,
                             tiles_per_batch=rows_per_batch // tm, ctx_row=ctx_row)
    return pl.pallas_call(
        kern,
        grid=(N // tm,),
        in_specs=[
            pl.BlockSpec((tm, D), lambda i: (i, 0)),
            pl.BlockSpec((None, None, R, D), lambda i: (l, 0, 0, 0)),
            pl.BlockSpec((None, None, R, D), lambda i: (l, 1, 0, 0)),
            pl.BlockSpec((1, D), lambda i: (0, 0)),
            pl.BlockSpec((D, U_WIDTH), lambda i: (0, 0)),
        ],
        out_specs=pl.BlockSpec((tm, U_WIDTH), lambda i: (i, 0)),
        out_shape=jax.ShapeDtypeStruct((N, U_WIDTH), BF16),
        compiler_params=_cparams(("arbitrary",)),
        name="inproj",
    )(x, mods, mods, norm_g.reshape(1, D), w_perm)


def _gla_chunks(work):
    Lc = GLA_CHUNK
    r = lax.broadcasted_iota(jnp.int32, (Lc, Lc), 0)
    c = lax.broadcasted_iota(jnp.int32, (Lc, Lc), 1)
    lane_head = lax.broadcasted_iota(jnp.int32, (Lc, GLA_QK), 1) // GLA_DK
    vlane_head = lax.broadcasted_iota(jnp.int32, (Lc, GLA_V), 1) // GLA_DV
    t_i = lax.broadcasted_iota(jnp.int32, (Lc, GLA_HEADS * Lc), 0)
    s_i = lax.broadcasted_iota(jnp.int32, (Lc, GLA_HEADS * Lc), 1) % Lc
    row_head = lax.broadcasted_iota(jnp.int32, (GLA_V, GLA_QK), 0) // GLA_DV
    col_head = lax.broadcasted_iota(jnp.int32, (GLA_V, GLA_QK), 1) // GLA_DK

    zs = [_dot(gl, up) + bias for (_, _, gl, up, bias, _, _) in work]
    bs = []
    for z, (_, _, _, _, _, _, forward) in zip(zs, work):
        la = (jnp.minimum(z, 0.0) - jnp.log(1.0 + jnp.exp(-jnp.abs(z)))) * (1.0 / GLA_TAU)
        tri = jnp.where((r >= c) if forward else (r <= c), 1.0, 0.0).astype(BF16)
        la_hi, la_lo = _split_bf16(la)
        bs.append(_dot(tri, la_hi) + _dot(tri, la_lo))
    mids = []
    for b, (qk, v, _, _, _, _, forward) in zip(bs, work):
        q = qk[:, :GLA_QK].astype(F32) * (GLA_DK ** -0.5)
        k = qk[:, GLA_QK:].astype(F32)
        i_last, i_ref = (Lc - 1, Lc // 2) if forward else (0, Lc - 1 - Lc // 2)
        b_last = b[i_last:i_last + 1, :]
        b_ref = b[i_ref:i_ref + 1, :]
        qd = (q * jnp.exp(b - b_ref)).astype(BF16)
        kd = (k * jnp.exp(b_ref - b)).astype(BF16)
        qe = (q * jnp.exp(b)).astype(BF16)
        kl = (k * jnp.exp(b_last - b)).astype(BF16)
        a = jnp.exp(b_last)
        kd_blk = jnp.concatenate([jnp.where(lane_head == h, kd, jnp.zeros_like(kd)) for h in range(GLA_HEADS)], axis=0)
        v_blk = jnp.concatenate([jnp.where(vlane_head == h, v, jnp.zeros_like(v)) for h in range(GLA_HEADS)], axis=0)
        mids.append((qd, kd_blk, qe, kl, a, v_blk))
    scs = [_dot_nt(qd, kd_blk) for (qd, kd_blk, _, _, _, _) in mids]
    dss = [_dot_tn(v, kl) for (_, v, _, _, _, _, _), (_, _, _, kl, _, _) in zip(work, mids)]
    outs = []
    for sc, ds, (_, _, qe, _, a, v_blk), (_, _, _, _, _, st_ref, forward) in zip(scs, dss, mids, work):
        sc = jnp.where((s_i <= t_i) if forward else (s_i >= t_i), sc, 0.0).astype(BF16)
        st = st_ref[...]
        outs.append(_dot(sc, v_blk) + _dot_nt(qe, st.astype(BF16)))
        st_ref[...] = a * st + jnp.where(row_head == col_head, ds, 0.0)
    return outs


def _gla_kernel(qkf_ref, vf_ref, glf_ref, qkb_ref, vb_ref, glb_ref, up_ref, bias_ref, s0f_ref, s0b_ref,
                of_ref, ob_ref, sf_ref, sb_ref, stf, stb, *, nsub):
    n = pl.program_id(1)

    @pl.when(n == 0)
    def _():
        stf[...] = s0f_ref[...]
        stb[...] = s0b_ref[...]

    def body(j, carry):
        work, rows = [], []
        for u in range(GLA_CHUNKS_PER_BODY):
            jj = j * GLA_CHUNKS_PER_BODY + u
            rf = pl.ds(pl.multiple_of(jj * GLA_CHUNK, GLA_CHUNK), GLA_CHUNK)
            rb = pl.ds(pl.multiple_of((nsub - 1 - jj) * GLA_CHUNK, GLA_CHUNK), GLA_CHUNK)
            work += [
                (qkf_ref[rf, :], vf_ref[rf, :], glf_ref[rf, :], up_ref[:, :GLA_QK], bias_ref[:, :GLA_QK], stf, True),
                (qkb_ref[rb, :], vb_ref[rb, :], glb_ref[rb, :], up_ref[:, GLA_QK:], bias_ref[:, GLA_QK:], stb, False)]
            rows += [(of_ref, rf), (ob_ref, rb)]
        for (ref, rr), o in zip(rows, _gla_chunks(work)):
            ref[rr, :] = o
        return carry

    lax.fori_loop(0, nsub // GLA_CHUNKS_PER_BODY, body, 0)

    @pl.when(n == pl.num_programs(1) - 1)
    def _():
        sf_ref[...] = stf[...]
        sb_ref[...] = stb[...]


def _gla_call(u, row_off, B, T, s0f, s0b, up, bias):
    tb = min(TB_GLA, T)
    nb = T // tb
    off = row_off // tb
    fwd = lambda b, n: off + b * nb + n
    bwd = lambda b, n: off + b * nb + (nb - 1 - n)
    cqk, cv, cgl = OFF_GQK // 512, OFF_GV // 512, OFF_GL // 128
    st_spec = pl.BlockSpec((None, GLA_V, GLA_QK), lambda b, n: (b, 0, 0))
    st_shape = jax.ShapeDtypeStruct((B, GLA_V, GLA_QK), F32)
    o_shape = jax.ShapeDtypeStruct((B * T, GLA_V), F32)
    return pl.pallas_call(
        functools.partial(_gla_kernel, nsub=tb // GLA_CHUNK),
        grid=(B, nb),
        in_specs=[
            pl.BlockSpec((tb, 512), lambda b, n: (fwd(b, n), cqk)),
            pl.BlockSpec((tb, 512), lambda b, n: (fwd(b, n), cv)),
            pl.BlockSpec((tb, 128), lambda b, n: (fwd(b, n), cgl)),
            pl.BlockSpec((tb, 512), lambda b, n: (bwd(b, n), cqk)),
            pl.BlockSpec((tb, 512), lambda b, n: (bwd(b, n), cv)),
            pl.BlockSpec((tb, 128), lambda b, n: (bwd(b, n), cgl)),
            pl.BlockSpec((128, 2 * GLA_QK), lambda b, n: (0, 0)),
            pl.BlockSpec((1, 2 * GLA_QK), lambda b, n: (0, 0)),
            st_spec, st_spec,
        ],
        out_specs=[
            pl.BlockSpec((tb, GLA_V), lambda b, n: (b * nb + n, 0)),
            pl.BlockSpec((tb, GLA_V), lambda b, n: (b * nb + (nb - 1 - n), 0)),
            st_spec, st_spec,
        ],
        out_shape=[o_shape, o_shape, st_shape, st_shape],
        scratch_shapes=[pltpu.VMEM((GLA_V, GLA_QK), F32), pltpu.VMEM((GLA_V, GLA_QK), F32)],
        compiler_params=_cparams(("arbitrary", "arbitrary")),
        name="gla_scan",
    )(u, u, u, u, u, u, up, bias, s0f, s0b)


def _group_ms64(x):
    i = lax.broadcasted_iota(jnp.int32, (LANES, LANES), 0) // ATT_DH
    j = lax.broadcasted_iota(jnp.int32, (LANES, LANES), 1) // ATT_DH
    bd = jnp.where(i == j, 1.0, 0.0).astype(BF16)
    hi, lo = _split_bf16(x * x)
    return (_dot(hi, bd) + _dot(lo, bd)) * (1.0 / ATT_DH)


def _rope128(x, cos, s1, s2):
    return x * cos + pltpu.roll(x, LANES - ROPE_AXIS_DIM // 2, axis=1) * s1 + pltpu.roll(x, ROPE_AXIS_DIM // 2, axis=1) * s2


def _att_kernel(*refs, T, C, tk, use_rope):
    kx, vx, qh = refs[-3:]
    if T:
        (q_ref, kvl_ref, kvc_ref, cq_ref, s1q_ref, s2q_ref, ck_ref, s1k_ref, s2k_ref, qg_ref, kg_ref,
         o_ref) = refs[:-3]
    else:
        q_ref, kvc_ref, qg_ref, kg_ref, o_ref = refs[:-3]
    i = pl.program_id(1)
    lo_half = lax.broadcasted_iota(jnp.int32, (1, LANES), 1) < ATT_DH

    def put_kv(rows, kv, rope_tabs):
        k = kv[:, :LANES].astype(F32)
        v = kv[:, LANES:].astype(F32)
        k = k * lax.rsqrt(_group_ms64(k) + NORM_EPS) * kg_ref[...]
        if rope_tabs is not None:
            k = _rope128(k, *rope_tabs)
        k_sw = pltpu.roll(k, ATT_DH, axis=1)
        v_sw = pltpu.roll(v, ATT_DH, axis=1)
        kx[0, rows, :] = jnp.where(lo_half, k, k_sw).astype(BF16)
        kx[1, rows, :] = jnp.where(lo_half, k_sw, k).astype(BF16)
        rid = lax.broadcasted_iota(jnp.int32, (ATT_V_ROWS, 1), 0)
        for g, vg in enumerate((v, v_sw)):
            vt = vg.T[:ATT_V_ROWS, :]
            vx[g, :, rows] = jnp.where(rid < ATT_DH, vt, jnp.where(rid == ATT_DH, 1.0, 0.0)).astype(BF16)

    @pl.when(i == 0)
    def _():
        if T:
            def body(j, carry):
                rows = pl.ds(pl.multiple_of(j * tk, tk), tk)
                put_kv(rows, kvl_ref[rows, :], (ck_ref[rows, :], s1k_ref[rows, :], s2k_ref[rows, :]))
                return carry
            lax.fori_loop(0, T // tk, body, 0)
        put_kv(pl.ds(T, C), kvc_ref[...], None)

    n_slab = ATT_Q // LANES
    tq = q_ref.shape[0]
    for s in range(n_slab):
        q = q_ref[:, s * LANES:(s + 1) * LANES].astype(F32)
        q = q * lax.rsqrt(_group_ms64(q) + NORM_EPS) * qg_ref[...]
        if use_rope:
            q = _rope128(q, cq_ref[...], s1q_ref[...], s2q_ref[...])
        q = (q * (ATT_DH ** -0.5 * LOG2_E)).astype(BF16)
        qh[2 * s] = jnp.where(lo_half, q, jnp.zeros_like(q))
        qh[2 * s + 1] = jnp.where(lo_half, jnp.zeros_like(q), q)

    heads_per_kv = ATT_HEADS // ATT_KV_HEADS
    lo_rows = lax.broadcasted_iota(jnp.int32, (LANES, 1), 0) < ATT_DH
    for g in range(ATT_KV_HEADS):
        heads = range(g * heads_per_kv, (g + 1) * heads_per_kv)

        ahead = ATT_SCORE_LOOKAHEAD

        def chunk(rows, carry):
            kc = kx[g, rows, :]
            vt = vx[g, :, rows]
            scores = {h: _dot_nt(kc, qh[h]) for h in heads[:ahead]}
            new = []
            for n, (h, (m, acc)) in enumerate(zip(heads, carry)):
                if n + ahead < len(heads):
                    scores[heads[n + ahead]] = _dot_nt(kc, qh[heads[n + ahead]])
                sc = scores.pop(h)
                m_new = jnp.maximum(m, jnp.max(sc, axis=0, keepdims=True))
                p = jnp.exp2(sc - m_new).astype(BF16)
                acc = jnp.exp2(m - m_new) * acc + _dot(vt, p)
                new.append((m_new, acc))
            return tuple(new)

        init = (jnp.full((1, tq), -jnp.inf, F32), jnp.zeros((ATT_V_ROWS, tq), F32))
        carry = (init,) * heads_per_kv
        if T:
            carry = lax.fori_loop(
                0, T // tk, lambda j, cr: chunk(pl.ds(pl.multiple_of(j * tk, tk), tk), cr), carry)
        carry = chunk(pl.ds(T, C), carry)
        for hi in range(0, heads_per_kv, 2):
            s = (g * heads_per_kv + hi) // 2
            halves = [a[:ATT_DH, :] / a[ATT_DH:ATT_DH + 1, :] for _, a in (carry[hi], carry[hi + 1])]
            o_ref[:, s * LANES:(s + 1) * LANES] = jnp.concatenate(halves, axis=0).T.astype(BF16)


def _att_call(u, q_row_off, B, Tq, T, C, ctx_row_off, tabs, qg, kg):
    tq = min(TQ_ATT, Tq)
    nq = Tq // tq
    tk = TK_ATT
    S = T + C
    hpk = ATT_HEADS // ATT_KV_HEADS
    cq, ckv = OFF_AQ // 512, OFF_AKV // 256
    qoff = q_row_off // tq
    coff = ctx_row_off // C
    g_spec = pl.BlockSpec((1, LANES), lambda b, i: (0, 0))
    q_spec = pl.BlockSpec((tq, ATT_Q), lambda b, i: (qoff + b * nq + i, cq))
    kvc_spec = pl.BlockSpec((C, 2 * ATT_KV), lambda b, i: (coff + b, ckv))
    if T:
        tq_tab = pl.BlockSpec((tq, LANES), lambda b, i: (i, 0))
        tk_tab = pl.BlockSpec((T, LANES), lambda b, i: (0, 0))
        in_specs = [q_spec, pl.BlockSpec((T, 2 * ATT_KV), lambda b, i: (b, ckv)), kvc_spec,
                    tq_tab, tq_tab, tq_tab, tk_tab, tk_tab, tk_tab, g_spec, g_spec]
        args = (u, u, u, tabs[0], tabs[1], tabs[2], tabs[0], tabs[1], tabs[2], qg, kg)
    else:
        in_specs = [q_spec, kvc_spec, g_spec, g_spec]
        args = (u, u, qg, kg)
    return pl.pallas_call(
        functools.partial(_att_kernel, T=T, C=C, tk=tk, use_rope=bool(T)),
        grid=(B, nq),
        in_specs=in_specs,
        out_specs=pl.BlockSpec((tq, ATT_Q), lambda b, i: (b * nq + i, 0)),
        out_shape=jax.ShapeDtypeStruct((B * Tq, ATT_Q), BF16),
        scratch_shapes=[pltpu.VMEM((ATT_KV_HEADS, S, LANES), BF16), pltpu.VMEM((ATT_KV_HEADS, ATT_V_ROWS, S), BF16),
                        pltpu.VMEM((ATT_HEADS, tq, LANES), BF16)],
        compiler_params=pltpu.CompilerParams(dimension_semantics=("arbitrary", "arbitrary"),
                                             vmem_limit_bytes=VMEM_LIMIT_BYTES, flags=ATT_FLAGS),
        name="gqa_lat" if T else "gqa_ctx",
    )(*args)


def _merge_kernel(*refs, n_seq_tiles, fill_tail, **kw):
    xo_ref, h2_ref, lg_ref = refs[-3:]
    if not fill_tail:
        _merge_body(*refs, **kw)
        return
    i = pl.program_id(0)

    @pl.when(i < n_seq_tiles)
    def _():
        _merge_body(*refs, **kw)

    @pl.when(i >= n_seq_tiles)
    def _():
        for r in (xo_ref, h2_ref, lg_ref):
            r[...] = jnp.zeros_like(r)


def _merge_body(*refs, T, tm, tiles_per_batch, row_base):
    (of_ref, ob_ref, gr_ref, pu_ref, pp_ref, pn_ref, gt_ref, ya_ref, x_ref, g1_ref, sh2_ref, sc2_ref,
     gng_ref, band_ref, pw_ref, ps_ref, wb_ref, wo_ref, n2g_ref, wrh_ref, wrl_ref, br_ref) = refs[:22]
    xo_ref, h2_ref, lg_ref = refs[-3:]
    i = pl.program_id(0)
    it = i % tiles_per_batch
    b = i // tiles_per_batch if row_base is None else row_base

    gt = gt_ref[...].astype(F32)
    z = _sigmoid(gt[:, D_MODEL:2 * D_MODEL]) * _dot(ya_ref[...], wb_ref[1])

    prev = jnp.where(it > 0, pp_ref[...], jnp.zeros_like(pp_ref[...]))
    nxt = jnp.where(it < tiles_per_batch - 1, pn_ref[...], jnp.zeros_like(pn_ref[...]))
    ext = jnp.concatenate([prev, pu_ref[...], nxt], axis=0)
    sub = 128
    n_sub = tm // sub
    ext_sub = [ext[r * sub:r * sub + sub + 2 * POOL_HALO, :] for r in range(n_sub)]
    groups = [slice(gi * POOL_GROUP, (gi + 1) * POOL_GROUP) for gi in range(len(POOL_WINDOWS))]
    wsums = [[_dot(band_ref[gi], e[:, gs]) for gi, gs in enumerate(groups)] for e in ext_sub]
    yp_rows = []
    for r, e in enumerate(ext_sub):
        t = it * tm + r * sub + lax.broadcasted_iota(jnp.int32, (sub, 1), 0)
        cols = []
        for gi, win in enumerate(POOL_WINDOWS):
            cnt = (jnp.minimum(t + win // 2, T) - jnp.maximum(t - win // 2, 0)).astype(F32)
            d = wsums[r][gi] / cnt - e[POOL_HALO:POOL_HALO + sub, groups[gi]].astype(F32)
            cols.append(_dot(d.astype(BF16), pw_ref[gi]))
        yp_rows.append(jnp.concatenate(cols, axis=1))
    y_pool = (jnp.concatenate(yp_rows, axis=0) * ps_ref[...]).astype(BF16)
    z = z + _sigmoid(gt[:, 2 * D_MODEL:]) * _dot(y_pool, wb_ref[2])

    o = of_ref[...] + ob_ref[...]
    gr = gr_ref[...].astype(F32)
    parts = []
    for h in range(GLA_HEADS):
        hs = slice(h * GLA_DV, (h + 1) * GLA_DV)
        oh = o[:, hs]
        ms = jnp.mean(oh * oh, axis=-1, keepdims=True)
        parts.append(oh * lax.rsqrt(ms + NORM_EPS) * gng_ref[...])
    y_gla = (jnp.concatenate(parts, axis=1) * _silu(gr)).astype(BF16)
    z = z + _sigmoid(gt[:, :D_MODEL]) * _dot(y_gla, wb_ref[0])
    y = _dot(z.astype(BF16), wo_ref[...])
    xn = x_ref[...] + g1_ref[pl.ds(b, 1), :] * y
    xo_ref[...] = xn
    ms = jnp.mean(xn * xn, axis=-1, keepdims=True)
    h2 = xn * lax.rsqrt(ms + NORM_EPS) * n2g_ref[...]
    h2 = h2 * (1.0 + sc2_ref[pl.ds(b, 1), :]) + sh2_ref[pl.ds(b, 1), :]
    _store_token_rows(h2_ref, h2)
    hh, hl = _split_bf16(h2)
    lg_ref[...] = _dot(hh, wrh_ref[...]) + _dot(hh, wrl_ref[...]) + _dot(hl, wrh_ref[...]) + br_ref[...]


def _merge_call(o_f, o_b, u, y_att, xin, x_row_off, B, T, mods, l, mod_row, wts, n_out_rows, prev_outs):
    tm = min(TM_MERGE, T)
    tpb = T // tm
    D = D_MODEL
    R = mods.shape[2]
    ro = x_row_off // tm
    ro16 = x_row_off // POOL_HALO
    r16 = tm // POOL_HALO
    n16 = u.shape[0] // POOL_HALO
    cgr, cpu = OFF_GR // 512, OFF_PU // 512
    n_seq_tiles = B * tpb
    n_grid = n_seq_tiles if prev_outs is not None else (n_out_rows - x_row_off) // tm
    fill_tail = n_grid > n_seq_tiles
    ic = lambda i: jnp.minimum(i, n_seq_tiles - 1)
    full = lambda shp: pl.BlockSpec(shp, lambda i: (0,) * len(shp))
    mod_spec = lambda k: pl.BlockSpec((None, None, R, D), lambda i: (l, k, 0, 0))
    in_specs = [
        pl.BlockSpec((tm, GLA_V), lambda i: (ic(i), 0)),
        pl.BlockSpec((tm, GLA_V), lambda i: (ic(i), 0)),
        pl.BlockSpec((tm, 512), lambda i: (ro + ic(i), cgr)),
        pl.BlockSpec((tm, 512), lambda i: (ro + ic(i), cpu)),
        pl.BlockSpec((POOL_HALO, 512), lambda i: (jnp.maximum(ro16 + ic(i) * r16 - 1, 0), cpu)),
        pl.BlockSpec((POOL_HALO, 512), lambda i: (jnp.minimum(ro16 + (ic(i) + 1) * r16, n16 - 1), cpu)),
        pl.BlockSpec((tm, N_BRANCH * D), lambda i: (ro + ic(i), 0)),
        pl.BlockSpec((tm, ATT_Q), lambda i: (ic(i), 0)),
        pl.BlockSpec((tm, D), lambda i: (ro + ic(i), 0)),
        mod_spec(2), mod_spec(3), mod_spec(4),
        full((1, GLA_DV)), full((4, 128, 128 + 2 * POOL_HALO)), full((4, POOL_GROUP, POOL_GROUP)), full((1, POOL_WIDTH)),
        full((N_BRANCH, BRANCH_WIDTH, D)), full((D, D)), full((1, D)), full((D, ROUTER_W)), full((D, ROUTER_W)),
        full((1, ROUTER_W)),
    ]
    args = [o_f, o_b, u, u, u, u, u, y_att, xin, mods, mods, mods, *wts]
    out_shape = [jax.ShapeDtypeStruct((n_out_rows, D), F32), jax.ShapeDtypeStruct((n_out_rows * TOK_ROWS, LANES), F32),
                 jax.ShapeDtypeStruct((n_out_rows, ROUTER_W), F32)]
    out_specs = [pl.BlockSpec((tm, D), lambda i: (ro + i, 0)), pl.BlockSpec((tm * TOK_ROWS, LANES), lambda i: (ro + i, 0)),
                 pl.BlockSpec((tm, ROUTER_W), lambda i: (ro + i, 0))]
    aliases = {}
    if prev_outs is not None:
        n_in = len(args)
        in_specs += [pl.BlockSpec(memory_space=pl.ANY)] * 3
        args += list(prev_outs)
        aliases = {n_in: 0, n_in + 1: 1, n_in + 2: 2}
    kern = functools.partial(_merge_kernel, n_seq_tiles=n_seq_tiles, fill_tail=fill_tail,
                             T=T, tm=tm, tiles_per_batch=tpb, row_base=mod_row)
    return pl.pallas_call(
        kern,
        grid=(n_grid,),
        in_specs=in_specs,
        out_specs=out_specs,
        out_shape=out_shape,
        input_output_aliases=aliases,
        compiler_params=_cparams(("arbitrary",)),
        name="merge",
    )(*args)


def _router_kernel(lg_ref, out_ref, cnt_ref, carry, pstart, *, tm, bm):
    ph = pl.program_id(0)
    i = pl.program_id(1)
    row = lax.broadcasted_iota(jnp.int32, (LANES, 1), 0)
    neg = -jnp.inf
    x = lg_ref[...].T
    colmax = lambda a: jnp.max(a, axis=0, keepdims=True)
    colmin = lambda a: jnp.min(a, axis=0, keepdims=True)
    colsum = lambda a: jnp.sum(a, axis=0, keepdims=True)

    @pl.when((ph == 0) & (i == 0))
    def _():
        carry[...] = jnp.zeros_like(carry)

    gl = jnp.where(row < N_GROUPS, x, neg)
    gmax = colmax(gl)
    gsum = colsum(jnp.exp(gl - gmax))
    grp = colmin(jnp.where(gl == gmax, row, LANES))
    e_row = row - N_GROUPS
    row_grp = sum((e_row >= EXP_PER_GROUP * k).astype(jnp.int32) for k in range(1, N_GROUPS))
    row_grp = jnp.where(e_row < 0, -1, jnp.where(e_row < N_EXPERTS, row_grp, -1))
    el = jnp.where(row_grp == grp, x, neg)
    emax = colmax(el)
    esum = colsum(jnp.exp(el - emax))
    i1 = colmin(jnp.where(el == emax, row, LANES))
    el2 = jnp.where(row == i1, neg, el)
    m2 = colmax(el2)
    i2 = colmin(jnp.where(el2 == m2, row, LANES))
    p1 = 1.0 / esum
    p2 = jnp.exp(m2 - emax) / esum
    pg = 1.0 / gsum
    w1 = pg * p1 / (p1 + p2)
    w2 = pg * p2 / (p1 + p2)
    oh1 = row == i1
    oh2 = row == i2
    oh = jnp.where(oh1, 1.0, jnp.where(oh2, 1.0, 0.0))

    @pl.when(ph == 0)
    def _():
        carry[...] += jnp.sum(oh, axis=1, keepdims=True)

    @pl.when((ph == 1) & (i == 0))
    def _():
        cnt = carry[...]
        cnt_ref[...] = jnp.broadcast_to(cnt, cnt_ref.shape)
        nb = jnp.floor((cnt + (bm - 1)) * (1.0 / bm))
        hi = jnp.floor(nb * (1.0 / 16.0))
        lo = nb - 16.0 * hi
        r = lax.broadcasted_iota(jnp.int32, (LANES, LANES), 0)
        c = lax.broadcasted_iota(jnp.int32, (LANES, LANES), 1)
        lower = jnp.where(c < r, 1.0, 0.0).astype(BF16)
        hib = jnp.broadcast_to(hi, (LANES, LANES)).astype(BF16)
        lob = jnp.broadcast_to(lo, (LANES, LANES)).astype(BF16)
        pre = 16.0 * _dot(lower, hib) + _dot(lower, lob)
        pstart[...] = pre[:, 0:1] * bm
        carry[...] = jnp.zeros_like(carry)

    @pl.when(ph == 1)
    def _():
        s_i = lax.broadcasted_iota(jnp.int32, (tm, tm), 0)
        t_i = lax.broadcasted_iota(jnp.int32, (tm, tm), 1)
        upper = jnp.where(s_i < t_i, 1.0, 0.0).astype(BF16)
        before = _dot(oh.astype(BF16), upper)
        slot = pstart[...] + carry[...] + before
        d1 = colsum(jnp.where(oh1, slot, 0.0))
        d2 = colsum(jnp.where(oh2, slot, 0.0))
        carry[...] += jnp.sum(oh, axis=1, keepdims=True)
        r8 = lax.broadcasted_iota(jnp.int32, (8, 1), 0)
        top = jnp.where(r8 == 0, d1, jnp.where(r8 == 1, d2, jnp.where(r8 == 2, w1, jnp.where(r8 == 3, w2, 0.0))))
        out_ref[...] = jnp.concatenate([top, jnp.zeros((LANES - 8, tm), F32)], axis=0).T


def _router_call(logits, bm):
    N = logits.shape[0]
    tm = TM_ROUTE
    return pl.pallas_call(
        functools.partial(_router_kernel, tm=tm, bm=bm),
        grid=(2, N // tm),
        in_specs=[pl.BlockSpec((tm, ROUTER_W), lambda p, i: (i, 0))],
        out_specs=[pl.BlockSpec((tm, LANES), lambda p, i: (i * p, 0)), pl.BlockSpec((LANES, LANES), lambda p, i: (0, 0))],
        out_shape=[jax.ShapeDtypeStruct((N, LANES), F32), jax.ShapeDtypeStruct((LANES, LANES), F32)],
        scratch_shapes=[pltpu.VMEM((LANES, 1), F32), pltpu.VMEM((LANES, 1), F32)],
        compiler_params=_cparams(("arbitrary", "arbitrary")),
        name="router",
    )(logits)


def _by_parity(i, fn):
    for p in range(2):
        pl.when(i % 2 == p)(functools.partial(fn, p))


def _dispatch_kernel(idx_hbm, h2_ref, xs_in, xs_hbm, idx_a, idx_b, isem, dsem, *, tm, n_tiles):
    del xs_in
    i = pl.program_id(0)
    idx_s = (idx_a, idx_b)

    def idx_copy(j, slot):
        return pltpu.make_async_copy(idx_hbm.at[j], idx_s[slot], isem.at[slot])

    def row_copy(r, slot):
        return pltpu.make_async_copy(h2_ref.at[r], xs_hbm.at[slot], dsem)

    @pl.when(i == 0)
    def _():
        idx_copy(0, 0).start()

    def step(p):
        @pl.when(i + 1 < n_tiles)
        def _():
            idx_copy(i + 1, 1 - p).start()

        idx_copy(i, p).wait()

        def issue(r0, c):
            for u in range(DMA_UNROLL):
                r = r0 * DMA_UNROLL + u
                for k in range(TOP_K):
                    row_copy(r, idx_s[p][0, TOP_K * r + k]).start()
            return c

        lax.fori_loop(0, tm // DMA_UNROLL, issue, 0)

    _by_parity(i, step)

    def drain(r0, c):
        for _ in range(TOP_K * DMA_UNROLL):
            row_copy(0, 0).wait()
        return c

    lax.fori_loop(0, tm // DMA_UNROLL, drain, 0)


def _dispatch_call(idx, h2, n_slots):
    n_tiles = idx.shape[0]
    tm = TM_DISP
    xs0 = jnp.zeros((n_slots, TOK_ROWS, LANES), F32)
    return pl.pallas_call(
        functools.partial(_dispatch_kernel, tm=tm, n_tiles=n_tiles),
        grid=(n_tiles,),
        in_specs=[pl.BlockSpec(memory_space=pl.ANY),
                  pl.BlockSpec((tm, TOK_ROWS, LANES), lambda i: (i, 0, 0)),
                  pl.BlockSpec(memory_space=pl.ANY)],
        out_specs=pl.BlockSpec(memory_space=pl.ANY),
        out_shape=jax.ShapeDtypeStruct((n_slots, TOK_ROWS, LANES), F32),
        input_output_aliases={2: 0},
        scratch_shapes=[pltpu.SMEM((1, TOP_K * tm), jnp.int32), pltpu.SMEM((1, TOP_K * tm), jnp.int32),
                        pltpu.SemaphoreType.DMA((2,)),
                        pltpu.SemaphoreType.DMA(())],
        compiler_params=pltpu.CompilerParams(dimension_semantics=("arbitrary",), vmem_limit_bytes=VMEM_LIMIT_BYTES,
                                             has_side_effects=True),
        name="dispatch",
    )(idx, h2, xs0)


def _expert_kernel(be_ref, nu_ref, xs_ref, wg_ref, wu_ref, wd_ref, ys_ref, wgb, wub, wdb, *, bm):
    i = pl.program_id(0)

    @pl.when((i == 0) | (be_ref[i] != be_ref[jnp.maximum(i - 1, 0)]))
    def _():
        wgb[...] = wg_ref[...].astype(BF16)
        wub[...] = wu_ref[...].astype(BF16)
        wdb[...] = wd_ref[...].astype(BF16)

    @pl.when(i < nu_ref[0])
    def _():
        part = bm // EXPERT_PARTS
        gu = []
        for c in range(EXPERT_PARTS):
            xb = _load_token_rows(xs_ref, part, c * part * TOK_ROWS).astype(BF16)
            gu.append((_dot(xb, wgb[...]), _dot(xb, wub[...])))
        for c, (g, up) in enumerate(gu):
            hmid = (_silu(g) * up).astype(BF16)
            _store_token_rows(ys_ref, _dot(hmid, wdb[...]), c * part * TOK_ROWS)

    @pl.when(i >= nu_ref[0])
    def _():
        ys_ref[...] = jnp.zeros_like(ys_ref)


def _expert_call(blk_expert, nused, xs, wg, wu, wd, l):
    bm = BM_MOE
    D = D_MODEL
    nblk = blk_expert.shape[0]
    w_in_spec = pl.BlockSpec((None, None, D, D_EXPERT), lambda i, be, nu: (l, be[i], 0, 0))
    grid_spec = pltpu.PrefetchScalarGridSpec(
        num_scalar_prefetch=2,
        grid=(nblk,),
        in_specs=[
            pl.BlockSpec((bm * TOK_ROWS, LANES), lambda i, be, nu: (jnp.minimum(i, nu[0] - 1), 0)),
            w_in_spec, w_in_spec,
            pl.BlockSpec((None, None, D_EXPERT, D), lambda i, be, nu: (l, be[i], 0, 0)),
        ],
        out_specs=pl.BlockSpec((bm * TOK_ROWS, LANES), lambda i, be, nu: (i, 0)),
        scratch_shapes=[pltpu.VMEM((D, D_EXPERT), BF16), pltpu.VMEM((D, D_EXPERT), BF16),
                        pltpu.VMEM((D_EXPERT, D), BF16)],
    )
    return pl.pallas_call(
        functools.partial(_expert_kernel, bm=bm),
        grid_spec=grid_spec,
        out_shape=jax.ShapeDtypeStruct(xs.shape, F32),
        compiler_params=_cparams(("arbitrary",)),
        name="experts",
    )(blk_expert, nused, xs, wg, wu, wd)


def _combine_kernel(idx_hbm, ys_hbm, x_ref, r_ref, g2_ref, fg_ref, o_ref, idx_a, idx_b, isem, ybuf, gsem, *,
                    tm, n_tiles, n_lat_tiles, tiles_per_batch, ctx_row, final):
    i = pl.program_id(0)
    idx_s = (idx_a, idx_b)

    def idx_copy(j, slot):
        return pltpu.make_async_copy(idx_hbm.at[j], idx_s[slot], isem.at[slot])

    def row_copy(slot, r, k, src):
        rows = pl.ds(pl.multiple_of((k * tm + r) * TOK_ROWS, TOK_ROWS), TOK_ROWS)
        return pltpu.make_async_copy(ys_hbm.at[src], ybuf.at[slot, rows, :], gsem.at[slot])

    def start_gather(slot):
        def issue(r0, c):
            for u in range(DMA_UNROLL):
                r = r0 * DMA_UNROLL + u
                for k in range(TOP_K):
                    row_copy(slot, r, k, idx_s[slot][0, TOP_K * r + k]).start()
            return c
        lax.fori_loop(0, tm // DMA_UNROLL, issue, 0)

    def wait_gather(slot):
        def drain(r0, c):
            for _ in range(TOP_K * DMA_UNROLL):
                row_copy(slot, 0, 0, 0).wait()
            return c
        lax.fori_loop(0, tm // DMA_UNROLL, drain, 0)

    @pl.when(i == 0)
    def _():
        idx_copy(0, 0).start()
        idx_copy(0, 0).wait()
        start_gather(0)
        if n_tiles > 1:
            idx_copy(1, 1).start()

    def step(p):
        @pl.when(i + 1 < n_tiles)
        def _():
            idx_copy(i + 1, 1 - p).wait()
            start_gather(1 - p)

        @pl.when(i + 2 < n_tiles)
        def _():
            idx_copy(i + 2, p).start()

        wait_gather(p)
        yb = ybuf.at[p]
        y0 = _load_token_rows(yb, tm, 0)
        y1 = _load_token_rows(yb, tm, tm * TOK_ROWS)
        m = r_ref[:, 2:3] * y0 + r_ref[:, 3:4] * y1
        b = jnp.where(i < n_lat_tiles, i // tiles_per_batch, ctx_row)
        xn = x_ref[...] + g2_ref[pl.ds(b, 1), :] * m
        if final:
            ms = jnp.mean(xn * xn, axis=-1, keepdims=True)
            xn = xn * lax.rsqrt(ms + NORM_EPS) * fg_ref[...]
        o_ref[...] = xn

    _by_parity(i, step)


def _combine_call(idx, ys, x_mid, route, mods, l, final_g, n_lat_rows, rows_per_batch, ctx_row, final):
    tm = TM_COMB
    D = D_MODEL
    R = mods.shape[2]
    n_rows = x_mid.shape[0]
    n_tiles = n_rows // tm
    kern = functools.partial(_combine_kernel, tm=tm, n_tiles=n_tiles, n_lat_tiles=n_lat_rows // tm,
                             tiles_per_batch=rows_per_batch // tm, ctx_row=ctx_row, final=final)
    return pl.pallas_call(
        kern,
        grid=(n_tiles,),
        in_specs=[
            pl.BlockSpec(memory_space=pl.ANY),
            pl.BlockSpec(memory_space=pl.ANY),
            pl.BlockSpec((tm, D), lambda i: (i, 0)),
            pl.BlockSpec((tm, LANES), lambda i: (i, 0)),
            pl.BlockSpec((None, None, R, D), lambda i: (l, 5, 0, 0)),
            pl.BlockSpec((1, D), lambda i: (0, 0)),
        ],
        out_specs=pl.BlockSpec((tm, D), lambda i: (i, 0)),
        out_shape=jax.ShapeDtypeStruct((n_rows, D), F32),
        scratch_shapes=[pltpu.SMEM((1, TOP_K * tm), jnp.int32), pltpu.SMEM((1, TOP_K * tm), jnp.int32),
                        pltpu.SemaphoreType.DMA((2,)),
                        pltpu.VMEM((2, TOP_K * tm * TOK_ROWS, LANES), F32), pltpu.SemaphoreType.DMA((2,))],
        compiler_params=_cparams(("arbitrary",)),
        name="combine",
    )(idx, ys, x_mid, route, mods, final_g.reshape(1, D))


def _permute_w_in(w_in):
    gq, gk, gv, gr, glf, glb, aq, ak, av, pu, gt = jnp.split(w_in, np.cumsum(IN_SIZES)[:-1].tolist(), axis=-1)
    pad = jnp.zeros(w_in.shape[:-1] + (U_WIDTH - sum(IN_SIZES),), w_in.dtype)
    return jnp.concatenate([gt, gq, gk, gv, gr, aq, pu, ak, av, glf, glb, pad], axis=-1).astype(BF16)


def _rope_tables(T):
    rows = T // GRID_W
    row = np.repeat(np.arange(rows), GRID_W).astype(np.float32)
    col = np.tile(np.arange(GRID_W), rows).astype(np.float32)
    inv = jnp.asarray(ROPE_THETA, F32) ** (-jnp.arange(0, ROPE_AXIS_DIM, 2, dtype=F32) / ROPE_AXIS_DIM)
    ang_r = jnp.asarray(row)[:, None] * inv
    ang_c = jnp.asarray(col)[:, None] * inv
    zero = jnp.zeros_like(ang_r)
    cos = jnp.concatenate([jnp.cos(ang_r)] * 2 + [jnp.cos(ang_c)] * 2, axis=1)
    s1 = jnp.concatenate([-jnp.sin(ang_r), zero, -jnp.sin(ang_c), zero], axis=1)
    s2 = jnp.concatenate([zero, jnp.sin(ang_r), zero, jnp.sin(ang_c)], axis=1)
    return tuple(jnp.concatenate([t, t], axis=1) for t in (cos, s1, s2))


def _pool_bands():
    i = np.arange(128)[:, None]
    j = np.arange(128 + 2 * POOL_HALO)[None, :]
    bands = [((j >= i + POOL_HALO - w // 2) & (j < i + POOL_HALO + w // 2)).astype(np.float32) for w in POOL_WINDOWS]
    return jnp.asarray(np.stack(bands), BF16)


def _block_table(counts, n_tok, bm):
    cnt = counts[N_GROUPS:N_GROUPS + N_EXPERTS, 0].astype(jnp.int32)
    pad_end = jnp.cumsum((cnt + bm - 1) // bm * bm)
    nblk = -(-(n_tok * TOP_K + N_EXPERTS * (bm - 1)) // bm)
    blk_start = jnp.arange(nblk, dtype=jnp.int32) * bm
    blk_expert = jnp.minimum(jnp.sum((pad_end[None, :] <= blk_start[:, None]).astype(jnp.int32), axis=1), N_EXPERTS - 1)
    nused = (pad_end[-1] // bm).astype(jnp.int32).reshape(1)
    return blk_expert.astype(jnp.int32), nused, nblk


def _slot_tiles(route, tm):
    n = route.shape[0]
    return route[:, :TOP_K].astype(jnp.int32).reshape(n // tm, 1, TOP_K * tm)


def kernel(x, c, ctx, c_ctx, w_mod, b_mod, norm1_g, norm2_g, w_in, gla_a_up_f, gla_a_bias_f, gla_a_up_b, gla_a_bias_b, gla_norm_g, att_qn_g, att_kn_g, pool_w, pool_scale, w_branch, w_out, moe_w_group, moe_b_group, moe_w_expert, moe_b_expert, moe_w_gate, moe_w_up, moe_w_down, final_g):
    B, T, D = x.shape
    C = ctx.shape[1]
    L = w_mod.shape[0]
    n_lat, n_ctx = B * T, B * C
    MOD_ROWS = 16
    assert D == D_MODEL and B < MOD_ROWS and T % TQ_ATT == 0 and C % GLA_CHUNK == 0

    s_in = jnp.concatenate([c, c_ctx[None], jnp.zeros((MOD_ROWS - B - 1, D), F32)], axis=0)
    mods = _mod_call(s_in, w_mod, b_mod)
    w_perm = _permute_w_in(w_in)
    tabs = _rope_tables(T)
    bands = _pool_bands()
    zero_state = jnp.zeros((B, GLA_V, GLA_QK), F32)

    xall = jnp.concatenate([x.reshape(n_lat, D), ctx.reshape(n_ctx, D)], axis=0)
    out = None
    for l in range(L):
        want_ctx = l < L - 1
        n_rows = xall.shape[0]
        u = _inproj_call(xall, mods, l, norm1_g[l], w_perm[l], n_lat, T, B)

        up = jnp.zeros((128, 2 * GLA_QK), F32)
        up = up.at[:GLA_RANK, :GLA_QK].set(gla_a_up_f[l]).at[GLA_RANK:2 * GLA_RANK, GLA_QK:].set(gla_a_up_b[l]).astype(BF16)
        bias = jnp.concatenate([gla_a_bias_f[l], gla_a_bias_b[l]]).reshape(1, 2 * GLA_QK)
        ofc, obc, sfc, sbc = _gla_call(u, n_lat, B, C, zero_state, zero_state, up, bias)
        of, ob, _, _ = _gla_call(u, 0, B, T, sfc, sbc, up, bias)

        qg = jnp.tile(att_qn_g[l], 2).reshape(1, LANES)
        kg = jnp.tile(att_kn_g[l], 2).reshape(1, LANES)
        ya = _att_call(u, 0, B, T, T, C, n_lat, tabs, qg, kg)

        wr = jnp.zeros((D, ROUTER_W), F32).at[:, :N_GROUPS].set(moe_w_group[l]).at[:, N_GROUPS:N_GROUPS + N_EXPERTS].set(moe_w_expert[l])
        wrh, wrl = _split_bf16(wr)
        br = jnp.zeros((1, ROUTER_W), F32).at[0, :N_GROUPS].set(moe_b_group[l]).at[0, N_GROUPS:N_GROUPS + N_EXPERTS].set(moe_b_expert[l])
        wts = (gla_norm_g[l].reshape(1, GLA_DV), bands, pool_w[l].astype(BF16), pool_scale[l].reshape(1, POOL_WIDTH),
               w_branch[l].astype(BF16), w_out[l].astype(BF16), norm2_g[l].reshape(1, D), wrh, wrl, br)
        n_tok = n_rows if want_ctx else n_lat
        outs = _merge_call(of, ob, u, ya, xall, 0, B, T, mods, l, None, wts, n_tok, None)
        if want_ctx:
            yac = _att_call(u, n_lat, B, C, 0, C, n_lat, None, qg, kg)
            outs = _merge_call(ofc, obc, u, yac, xall, n_lat, B, C, mods, l, B, wts, n_tok, outs)
        x_mid, h2, logits = outs

        route, counts = _router_call(logits, BM_MOE)
        blk_expert, nused, nblk = _block_table(counts, n_tok, BM_MOE)
        n_slots = nblk * BM_MOE
        xs = _dispatch_call(_slot_tiles(route, TM_DISP), h2.reshape(n_tok, TOK_ROWS, LANES), n_slots)
        ys = _expert_call(blk_expert, nused, xs.reshape(n_slots * TOK_ROWS, LANES), moe_w_gate, moe_w_up, moe_w_down, l)
        xall = _combine_call(_slot_tiles(route, TM_COMB), ys.reshape(n_slots, TOK_ROWS, LANES), x_mid, route, mods, l,
                             final_g, n_lat, T, B, final=not want_ctx)
    return xall[:n_lat].reshape(B, T, D)
```

```python
import functools

import numpy as np
import jax
import jax.numpy as jnp
from jax import lax
from jax.experimental import pallas as pl
from jax.experimental.pallas import tpu as pltpu

F32 = jnp.float32
BF16 = jnp.bfloat16

VMEM_LIMIT_BYTES = 56 * 1024 * 1024
LANES = 128

D_MODEL = 1024
GRID_W = 64
NORM_EPS = 1e-6
GLA_HEADS, GLA_DK, GLA_DV, GLA_RANK, GLA_TAU, GLA_CHUNK = 4, 64, 128, 16, 16.0, 64
GLA_QK, GLA_V = GLA_HEADS * GLA_DK, GLA_HEADS * GLA_DV
ATT_HEADS, ATT_KV_HEADS, ATT_DH = 8, 2, 64
ROPE_THETA, ROPE_AXIS_DIM = 10000.0, 32
ATT_Q, ATT_KV = ATT_HEADS * ATT_DH, ATT_KV_HEADS * ATT_DH
POOL_WINDOWS, POOL_GROUP = (2, 4, 8, 16), 128
POOL_WIDTH = POOL_GROUP * len(POOL_WINDOWS)
POOL_HALO = 16
N_BRANCH, BRANCH_WIDTH = 3, 512
N_GROUPS, EXP_PER_GROUP, TOP_K, D_EXPERT = 4, 8, 2, 512
N_EXPERTS = N_GROUPS * EXP_PER_GROUP
IN_SIZES = (GLA_QK, GLA_QK, GLA_V, GLA_V, GLA_RANK, GLA_RANK, ATT_Q, ATT_KV, ATT_KV, POOL_WIDTH, N_BRANCH * D_MODEL)

OFF_GT, OFF_GQK, OFF_GV, OFF_GR, OFF_AQ, OFF_PU, OFF_AKV, OFF_GL = 0, 3072, 3584, 4096, 4608, 5120, 5632, 5888
U_WIDTH = 6144
U_CHUNK = 512

TM_IN = 512
TB_GLA = 512
GLA_CHUNKS_PER_BODY = 4
TQ_ATT = 1024
TK_ATT = 512
ATT_SUB_ROWS = 64
LOG2_E = 1.4426950408889634
ATT_FLAGS = {}
ATT_V_ROWS = 80
ATT_SCORE_LOOKAHEAD = 2
TM_MERGE = 512
BM_MOE = 512
EXPERT_PARTS = 2
TM_ROUTE = 512
TM_DISP = 512
TM_COMB = 512
ROUTER_W = 128
TOK_ROWS = D_MODEL // LANES
DMA_UNROLL = 8


def _cparams(sem):
    return pltpu.CompilerParams(dimension_semantics=sem, vmem_limit_bytes=VMEM_LIMIT_BYTES)


def _split_bf16(a):
    hi = a.astype(BF16)
    lo = (a - hi.astype(F32)).astype(BF16)
    return hi, lo


def _dot(a, b):
    return jnp.dot(a, b, preferred_element_type=F32)


def _dot_nt(a, b):
    return lax.dot_general(a, b, (((1,), (1,)), ((), ())), preferred_element_type=F32)


def _dot_tn(a, b):
    return lax.dot_general(a, b, (((0,), (0,)), ((), ())), preferred_element_type=F32)


def _dot3(a, b):
    ah, al = _split_bf16(a)
    bh, bl = _split_bf16(b)
    return _dot(ah, bh) + _dot(ah, bl) + _dot(al, bh)


def _load_token_rows(ref, n_tok, row0=0):
    return jnp.concatenate([ref[pl.ds(row0 + s, n_tok, stride=TOK_ROWS), :] for s in range(TOK_ROWS)], axis=1)


def _store_token_rows(ref, val, row0=0):
    n_tok = val.shape[0]
    for s in range(TOK_ROWS):
        ref[pl.ds(row0 + s, n_tok, stride=TOK_ROWS), :] = val[:, s * LANES:(s + 1) * LANES]


def _sigmoid(x):
    return 1.0 / (1.0 + jnp.exp(-x))


def _silu(x):
    return x * _sigmoid(x)


def _mod_kernel(s_ref, w_ref, b_ref, o_ref):
    s = _silu(s_ref[...])
    o_ref[...] = _dot3(s, w_ref[...]) + b_ref[...]


def _mod_call(s_in, w_mod, b_mod):
    L, D, _ = w_mod.shape
    R = s_in.shape[0]
    return pl.pallas_call(
        _mod_kernel,
        grid=(L, 6),
        in_specs=[
            pl.BlockSpec((R, D), lambda l, j: (0, 0)),
            pl.BlockSpec((None, D, D), lambda l, j: (l, 0, j)),
            pl.BlockSpec((None, None, 1, D), lambda l, j: (l, j, 0, 0)),
        ],
        out_specs=pl.BlockSpec((None, None, R, D), lambda l, j: (l, j, 0, 0)),
        out_shape=jax.ShapeDtypeStruct((L, 6, R, D), F32),
        compiler_params=_cparams(("arbitrary", "arbitrary")),
        name="mod_table",
    )(s_in, w_mod, b_mod.reshape(L, 6, 1, D))


def _inproj_kernel(x_ref, sh_ref, sc_ref, g_ref, w_ref, o_ref, *, n_lat_tiles, tiles_per_batch, ctx_row):
    i = pl.program_id(0)
    b = jnp.where(i < n_lat_tiles, i // tiles_per_batch, ctx_row)
    x = x_ref[...]
    ms = jnp.mean(x * x, axis=-1, keepdims=True)
    h = x * lax.rsqrt(ms + NORM_EPS) * g_ref[...]
    h = h * (1.0 + sc_ref[pl.ds(b, 1), :]) + sh_ref[pl.ds(b, 1), :]
    hb = h.astype(BF16)
    for c in range(U_WIDTH // U_CHUNK):
        cs = slice(c * U_CHUNK, (c + 1) * U_CHUNK)
        o_ref[:, cs] = _dot(hb, w_ref[:, cs]).astype(BF16)


def _inproj_call(x, mods, l, norm_g, w_perm, n_lat_rows, rows_per_batch, ctx_row):
    N, D = x.shape
    R = mods.shape[2]
    tm = TM_IN
    kern = functools.partial(_inproj_kernel, n_lat_tiles=n_lat_rows // tm,
                             tiles_per_batch=rows_per_batch // tm, ctx_row=ctx_row)
    return pl.pallas_call(
        kern,
        grid=(N // tm,),
        in_specs=[
            pl.BlockSpec((tm, D), lambda i: (i, 0)),
            pl.BlockSpec((None, None, R, D), lambda i: (l, 0, 0, 0)),
            pl.BlockSpec((None, None, R, D), lambda i: (l, 1, 0, 0)),
            pl.BlockSpec((1, D), lambda i: (0, 0)),
            pl.BlockSpec((D, U_WIDTH), lambda i: (0, 0)),
        ],
        out_specs=pl.BlockSpec((tm, U_WIDTH), lambda i: (i, 0)),
        out_shape=jax.ShapeDtypeStruct((N, U_WIDTH), BF16),
        compiler_params=_cparams(("arbitrary",)),
        name="inproj",
    )(x, mods, mods, norm_g.reshape(1, D), w_perm)


def _gla_chunks(work):
    Lc = GLA_CHUNK
    r = lax.broadcasted_iota(jnp.int32, (Lc, Lc), 0)
    c = lax.broadcasted_iota(jnp.int32, (Lc, Lc), 1)
    lane_head = lax.broadcasted_iota(jnp.int32, (Lc, GLA_QK), 1) // GLA_DK
    vlane_head = lax.broadcasted_iota(jnp.int32, (Lc, GLA_V), 1) // GLA_DV
    t_i = lax.broadcasted_iota(jnp.int32, (Lc, GLA_HEADS * Lc), 0)
    s_i = lax.broadcasted_iota(jnp.int32, (Lc, GLA_HEADS * Lc), 1) % Lc
    row_head = lax.broadcasted_iota(jnp.int32, (GLA_V, GLA_QK), 0) // GLA_DV
    col_head = lax.broadcasted_iota(jnp.int32, (GLA_V, GLA_QK), 1) // GLA_DK

    zs = [_dot(gl, up) + bias for (_, _, gl, up, bias, _, _) in work]
    bs = []
    for z, (_, _, _, _, _, _, forward) in zip(zs, work):
        la = (jnp.minimum(z, 0.0) - jnp.log(1.0 + jnp.exp(-jnp.abs(z)))) * (1.0 / GLA_TAU)
        tri = jnp.where((r >= c) if forward else (r <= c), 1.0, 0.0).astype(BF16)
        la_hi, la_lo = _split_bf16(la)
        bs.append(_dot(tri, la_hi) + _dot(tri, la_lo))
    mids = []
    for b, (qk, v, _, _, _, _, forward) in zip(bs, work):
        q = qk[:, :GLA_QK].astype(F32) * (GLA_DK ** -0.5)
        k = qk[:, GLA_QK:].astype(F32)
        i_last, i_ref = (Lc - 1, Lc // 2) if forward else (0, Lc - 1 - Lc // 2)
        b_last = b[i_last:i_last + 1, :]
        b_ref = b[i_ref:i_ref + 1, :]
        qd = (q * jnp.exp(b - b_ref)).astype(BF16)
        kd = (k * jnp.exp(b_ref - b)).astype(BF16)
        qe = (q * jnp.exp(b)).astype(BF16)
        kl = (k * jnp.exp(b_last - b)).astype(BF16)
        a = jnp.exp(b_last)
        kd_blk = jnp.concatenate([jnp.where(lane_head == h, kd, jnp.zeros_like(kd)) for h in range(GLA_HEADS)], axis=0)
        v_blk = jnp.concatenate([jnp.where(vlane_head == h, v, jnp.zeros_like(v)) for h in range(GLA_HEADS)], axis=0)
        mids.append((qd, kd_blk, qe, kl, a, v_blk))
    scs = [_dot_nt(qd, kd_blk) for (qd, kd_blk, _, _, _, _) in mids]
    dss = [_dot_tn(v, kl) for (_, v, _, _, _, _, _), (_, _, _, kl, _, _) in zip(work, mids)]
    outs = []
    for sc, ds, (_, _, qe, _, a, v_blk), (_, _, _, _, _, st_ref, forward) in zip(scs, dss, mids, work):
        sc = jnp.where((s_i <= t_i) if forward else (s_i >= t_i), sc, 0.0).astype(BF16)
        st = st_ref[...]
        outs.append(_dot(sc, v_blk) + _dot_nt(qe, st.astype(BF16)))
        st_ref[...] = a * st + jnp.where(row_head == col_head, ds, 0.0)
    return outs


def _gla_kernel(qkf_ref, vf_ref, glf_ref, qkb_ref, vb_ref, glb_ref, up_ref, bias_ref, s0f_ref, s0b_ref,
                of_ref, ob_ref, sf_ref, sb_ref, stf, stb, *, nsub):
    n = pl.program_id(1)

    @pl.when(n == 0)
    def _():
        stf[...] = s0f_ref[...]
        stb[...] = s0b_ref[...]

    def body(j, carry):
        work, rows = [], []
        for u in range(GLA_CHUNKS_PER_BODY):
            jj = j * GLA_CHUNKS_PER_BODY + u
            rf = pl.ds(pl.multiple_of(jj * GLA_CHUNK, GLA_CHUNK), GLA_CHUNK)
            rb = pl.ds(pl.multiple_of((nsub - 1 - jj) * GLA_CHUNK, GLA_CHUNK), GLA_CHUNK)
            work += [
                (qkf_ref[rf, :], vf_ref[rf, :], glf_ref[rf, :], up_ref[:, :GLA_QK], bias_ref[:, :GLA_QK], stf, True),
                (qkb_ref[rb, :], vb_ref[rb, :], glb_ref[rb, :], up_ref[:, GLA_QK:], bias_ref[:, GLA_QK:], stb, False)]
            rows += [(of_ref, rf), (ob_ref, rb)]
        for (ref, rr), o in zip(rows, _gla_chunks(work)):
            ref[rr, :] = o
        return carry

    lax.fori_loop(0, nsub // GLA_CHUNKS_PER_BODY, body, 0)

    @pl.when(n == pl.num_programs(1) - 1)
    def _():
        sf_ref[...] = stf[...]
        sb_ref[...] = stb[...]


def _gla_call(u, row_off, B, T, s0f, s0b, up, bias):
    tb = min(TB_GLA, T)
    nb = T // tb
    off = row_off // tb
    fwd = lambda b, n: off + b * nb + n
    bwd = lambda b, n: off + b * nb + (nb - 1 - n)
    cqk, cv, cgl = OFF_GQK // 512, OFF_GV // 512, OFF_GL // 128
    st_spec = pl.BlockSpec((None, GLA_V, GLA_QK), lambda b, n: (b, 0, 0))
    st_shape = jax.ShapeDtypeStruct((B, GLA_V, GLA_QK), F32)
    o_shape = jax.ShapeDtypeStruct((B * T, GLA_V), F32)
    return pl.pallas_call(
        functools.partial(_gla_kernel, nsub=tb // GLA_CHUNK),
        grid=(B, nb),
        in_specs=[
            pl.BlockSpec((tb, 512), lambda b, n: (fwd(b, n), cqk)),
            pl.BlockSpec((tb, 512), lambda b, n: (fwd(b, n), cv)),
            pl.BlockSpec((tb, 128), lambda b, n: (fwd(b, n), cgl)),
            pl.BlockSpec((tb, 512), lambda b, n: (bwd(b, n), cqk)),
            pl.BlockSpec((tb, 512), lambda b, n: (bwd(b, n), cv)),
            pl.BlockSpec((tb, 128), lambda b, n: (bwd(b, n), cgl)),
            pl.BlockSpec((128, 2 * GLA_QK), lambda b, n: (0, 0)),
            pl.BlockSpec((1, 2 * GLA_QK), lambda b, n: (0, 0)),
            st_spec, st_spec,
        ],
        out_specs=[
            pl.BlockSpec((tb, GLA_V), lambda b, n: (b * nb + n, 0)),
            pl.BlockSpec((tb, GLA_V), lambda b, n: (b * nb + (nb - 1 - n), 0)),
            st_spec, st_spec,
        ],
        out_shape=[o_shape, o_shape, st_shape, st_shape],
        scratch_shapes=[pltpu.VMEM((GLA_V, GLA_QK), F32), pltpu.VMEM((GLA_V, GLA_QK), F32)],
        compiler_params=_cparams(("arbitrary", "arbitrary")),
        name="gla_scan",
    )(u, u, u, u, u, u, up, bias, s0f, s0b)


def _group_ms64(x):
    i = lax.broadcasted_iota(jnp.int32, (LANES, LANES), 0) // ATT_DH
    j = lax.broadcasted_iota(jnp.int32, (LANES, LANES), 1) // ATT_DH
    bd = jnp.where(i == j, 1.0, 0.0).astype(BF16)
    hi, lo = _split_bf16(x * x)
    return (_dot(hi, bd) + _dot(lo, bd)) * (1.0 / ATT_DH)


def _rope128(x, cos, s1, s2):
    return x * cos + pltpu.roll(x, LANES - ROPE_AXIS_DIM // 2, axis=1) * s1 + pltpu.roll(x, ROPE_AXIS_DIM // 2, axis=1) * s2


def _att_kernel(*refs, T, C, tk, use_rope):
    kx, vx, qh = refs[-3:]
    if T:
        (q_ref, kvl_ref, kvc_ref, cq_ref, s1q_ref, s2q_ref, ck_ref, s1k_ref, s2k_ref, qg_ref, kg_ref,
         o_ref) = refs[:-3]
    else:
        q_ref, kvc_ref, qg_ref, kg_ref, o_ref = refs[:-3]
    i = pl.program_id(1)
    lo_half = lax.broadcasted_iota(jnp.int32, (1, LANES), 1) < ATT_DH

    def put_kv(rows, kv, rope_tabs):
        k = kv[:, :LANES].astype(F32)
        v = kv[:, LANES:].astype(F32)
        k = k * lax.rsqrt(_group_ms64(k) + NORM_EPS) * kg_ref[...]
        if rope_tabs is not None:
            k = _rope128(k, *rope_tabs)
        k_sw = pltpu.roll(k, ATT_DH, axis=1)
        v_sw = pltpu.roll(v, ATT_DH, axis=1)
        kx[0, rows, :] = jnp.where(lo_half, k, k_sw).astype(BF16)
        kx[1, rows, :] = jnp.where(lo_half, k_sw, k).astype(BF16)
        rid = lax.broadcasted_iota(jnp.int32, (ATT_V_ROWS, 1), 0)
        for g, vg in enumerate((v, v_sw)):
            vt = vg.T[:ATT_V_ROWS, :]
            vx[g, :, rows] = jnp.where(rid < ATT_DH, vt, jnp.where(rid == ATT_DH, 1.0, 0.0)).astype(BF16)

    @pl.when(i == 0)
    def _():
        if T:
            def body(j, carry):
                rows = pl.ds(pl.multiple_of(j * tk, tk), tk)
                put_kv(rows, kvl_ref[rows, :], (ck_ref[rows, :], s1k_ref[rows, :], s2k_ref[rows, :]))
                return carry
            lax.fori_loop(0, T // tk, body, 0)
        put_kv(pl.ds(T, C), kvc_ref[...], None)

    n_slab = ATT_Q // LANES
    tq = q_ref.shape[0]
    for s in range(n_slab):
        q = q_ref[:, s * LANES:(s + 1) * LANES].astype(F32)
        q = q * lax.rsqrt(_group_ms64(q) + NORM_EPS) * qg_ref[...]
        if use_rope:
            q = _rope128(q, cq_ref[...], s1q_ref[...], s2q_ref[...])
        q = (q * (ATT_DH ** -0.5 * LOG2_E)).astype(BF16)
        qh[2 * s] = jnp.where(lo_half, q, jnp.zeros_like(q))
        qh[2 * s + 1] = jnp.where(lo_half, jnp.zeros_like(q), q)

    heads_per_kv = ATT_HEADS // ATT_KV_HEADS
    lo_rows = lax.broadcasted_iota(jnp.int32, (LANES, 1), 0) < ATT_DH
    for g in range(ATT_KV_HEADS):
        heads = range(g * heads_per_kv, (g + 1) * heads_per_kv)

        ahead = ATT_SCORE_LOOKAHEAD

        def chunk(rows, carry):
            kc = kx[g, rows, :]
            vt = vx[g, :, rows]
            scores = {h: _dot_nt(kc, qh[h]) for h in heads[:ahead]}
            new = []
            for n, (h, (m, acc)) in enumerate(zip(heads, carry)):
                if n + ahead < len(heads):
                    scores[heads[n + ahead]] = _dot_nt(kc, qh[heads[n + ahead]])
                sc = scores.pop(h)
                m_new = jnp.maximum(m, jnp.max(sc, axis=0, keepdims=True))
                p = jnp.exp2(sc - m_new).astype(BF16)
                acc = jnp.exp2(m - m_new) * acc + _dot(vt, p)
                new.append((m_new, acc))
            return tuple(new)

        init = (jnp.full((1, tq), -jnp.inf, F32), jnp.zeros((ATT_V_ROWS, tq), F32))
        carry = (init,) * heads_per_kv
        if T:
            carry = lax.fori_loop(
                0, T // tk, lambda j, cr: chunk(pl.ds(pl.multiple_of(j * tk, tk), tk), cr), carry)
        carry = chunk(pl.ds(T, C), carry)
        for hi in range(0, heads_per_kv, 2):
            s = (g * heads_per_kv + hi) // 2
            halves = [a[:ATT_DH, :] / a[ATT_DH:ATT_DH + 1, :] for _, a in (carry[hi], carry[hi + 1])]
            o_ref[:, s * LANES:(s + 1) * LANES] = jnp.concatenate(halves, axis=0).T.astype(BF16)


def _att_call(u, q_row_off, B, Tq, T, C, ctx_row_off, tabs, qg, kg):
    tq = min(TQ_ATT, Tq)
    nq = Tq // tq
    tk = TK_ATT
    S = T + C
    hpk = ATT_HEADS // ATT_KV_HEADS
    cq, ckv = OFF_AQ // 512, OFF_AKV // 256
    qoff = q_row_off // tq
    coff = ctx_row_off // C
    g_spec = pl.BlockSpec((1, LANES), lambda b, i: (0, 0))
    q_spec = pl.BlockSpec((tq, ATT_Q), lambda b, i: (qoff + b * nq + i, cq))
    kvc_spec = pl.BlockSpec((C, 2 * ATT_KV), lambda b, i: (coff + b, ckv))
    if T:
        tq_tab = pl.BlockSpec((tq, LANES), lambda b, i: (i, 0))
        tk_tab = pl.BlockSpec((T, LANES), lambda b, i: (0, 0))
        in_specs = [q_spec, pl.BlockSpec((T, 2 * ATT_KV), lambda b, i: (b, ckv)), kvc_spec,
                    tq_tab, tq_tab, tq_tab, tk_tab, tk_tab, tk_tab, g_spec, g_spec]
        args = (u, u, u, tabs[0], tabs[1], tabs[2], tabs[0], tabs[1], tabs[2], qg, kg)
    else:
        in_specs = [q_spec, kvc_spec, g_spec, g_spec]
        args = (u, u, qg, kg)
    return pl.pallas_call(
        functools.partial(_att_kernel, T=T, C=C, tk=tk, use_rope=bool(T)),
        grid=(B, nq),
        in_specs=in_specs,
        out_specs=pl.BlockSpec((tq, ATT_Q), lambda b, i: (b * nq + i, 0)),
        out_shape=jax.ShapeDtypeStruct((B * Tq, ATT_Q), BF16),
        scratch_shapes=[pltpu.VMEM((ATT_KV_HEADS, S, LANES), BF16), pltpu.VMEM((ATT_KV_HEADS, ATT_V_ROWS, S), BF16),
                        pltpu.VMEM((ATT_HEADS, tq, LANES), BF16)],
        compiler_params=pltpu.CompilerParams(dimension_semantics=("arbitrary", "arbitrary"),
                                             vmem_limit_bytes=VMEM_LIMIT_BYTES, flags=ATT_FLAGS),
        name="gqa_lat" if T else "gqa_ctx",
    )(*args)


def _merge_kernel(*refs, n_seq_tiles, fill_tail, **kw):
    xo_ref, h2_ref, lg_ref = refs[-3:]
    if not fill_tail:
        _merge_body(*refs, **kw)
        return
    i = pl.program_id(0)

    @pl.when(i < n_seq_tiles)
    def _():
        _merge_body(*refs, **kw)

    @pl.when(i >= n_seq_tiles)
    def _():
        for r in (xo_ref, h2_ref, lg_ref):
            r[...] = jnp.zeros_like(r)


def _merge_body(*refs, T, tm, tiles_per_batch, row_base):
    (of_ref, ob_ref, gr_ref, pu_ref, pp_ref, pn_ref, gt_ref, ya_ref, x_ref, g1_ref, sh2_ref, sc2_ref,
     gng_ref, band_ref, pw_ref, ps_ref, wb_ref, wo_ref, n2g_ref, wrh_ref, wrl_ref, br_ref) = refs[:22]
    xo_ref, h2_ref, lg_ref = refs[-3:]
    i = pl.program_id(0)
    it = i % tiles_per_batch
    b = i // tiles_per_batch if row_base is None else row_base

    gt = gt_ref[...].astype(F32)
    z = _sigmoid(gt[:, D_MODEL:2 * D_MODEL]) * _dot(ya_ref[...], wb_ref[1])

    prev = jnp.where(it > 0, pp_ref[...], jnp.zeros_like(pp_ref[...]))
    nxt = jnp.where(it < tiles_per_batch - 1, pn_ref[...], jnp.zeros_like(pn_ref[...]))
    ext = jnp.concatenate([prev, pu_ref[...], nxt], axis=0)
    sub = 128
    n_sub = tm // sub
    ext_sub = [ext[r * sub:r * sub + sub + 2 * POOL_HALO, :] for r in range(n_sub)]
    groups = [slice(gi * POOL_GROUP, (gi + 1) * POOL_GROUP) for gi in range(len(POOL_WINDOWS))]
    wsums = [[_dot(band_ref[gi], e[:, gs]) for gi, gs in enumerate(groups)] for e in ext_sub]
    yp_rows = []
    for r, e in enumerate(ext_sub):
        t = it * tm + r * sub + lax.broadcasted_iota(jnp.int32, (sub, 1), 0)
        cols = []
        for gi, win in enumerate(POOL_WINDOWS):
            cnt = (jnp.minimum(t + win // 2, T) - jnp.maximum(t - win // 2, 0)).astype(F32)
            d = wsums[r][gi] / cnt - e[POOL_HALO:POOL_HALO + sub, groups[gi]].astype(F32)
            cols.append(_dot(d.astype(BF16), pw_ref[gi]))
        yp_rows.append(jnp.concatenate(cols, axis=1))
    y_pool = (jnp.concatenate(yp_rows, axis=0) * ps_ref[...]).astype(BF16)
    z = z + _sigmoid(gt[:, 2 * D_MODEL:]) * _dot(y_pool, wb_ref[2])

    o = of_ref[...] + ob_ref[...]
    gr = gr_ref[...].astype(F32)
    parts = []
    for h in range(GLA_HEADS):
        hs = slice(h * GLA_DV, (h + 1) * GLA_DV)
        oh = o[:, hs]
        ms = jnp.mean(oh * oh, axis=-1, keepdims=True)
        parts.append(oh * lax.rsqrt(ms + NORM_EPS) * gng_ref[...])
    y_gla = (jnp.concatenate(parts, axis=1) * _silu(gr)).astype(BF16)
    z = z + _sigmoid(gt[:, :D_MODEL]) * _dot(y_gla, wb_ref[0])
    y = _dot(z.astype(BF16), wo_ref[...])
    xn = x_ref[...] + g1_ref[pl.ds(b, 1), :] * y
    xo_ref[...] = xn
    ms = jnp.mean(xn * xn, axis=-1, keepdims=True)
    h2 = xn * lax.rsqrt(ms + NORM_EPS) * n2g_ref[...]
    h2 = h2 * (1.0 + sc2_ref[pl.ds(b, 1), :]) + sh2_ref[pl.ds(b, 1), :]
    _store_token_rows(h2_ref, h2)
    hh, hl = _split_bf16(h2)
    lg_ref[...] = _dot(hh, wrh_ref[...]) + _dot(hh, wrl_ref[...]) + _dot(hl, wrh_ref[...]) + br_ref[...]


def _merge_call(o_f, o_b, u, y_att, xin, x_row_off, B, T, mods, l, mod_row, wts, n_out_rows, prev_outs):
    tm = min(TM_MERGE, T)
    tpb = T // tm
    D = D_MODEL
    R = mods.shape[2]
    ro = x_row_off // tm
    ro16 = x_row_off // POOL_HALO
    r16 = tm // POOL_HALO
    n16 = u.shape[0] // POOL_HALO
    cgr, cpu = OFF_GR // 512, OFF_PU // 512
    n_seq_tiles = B * tpb
    n_grid = n_seq_tiles if prev_outs is not None else (n_out_rows - x_row_off) // tm
    fill_tail = n_grid > n_seq_tiles
    ic = lambda i: jnp.minimum(i, n_seq_tiles - 1)
    full = lambda shp: pl.BlockSpec(shp, lambda i: (0,) * len(shp))
    mod_spec = lambda k: pl.BlockSpec((None, None, R, D), lambda i: (l, k, 0, 0))
    in_specs = [
        pl.BlockSpec((tm, GLA_V), lambda i: (ic(i), 0)),
        pl.BlockSpec((tm, GLA_V), lambda i: (ic(i), 0)),
        pl.BlockSpec((tm, 512), lambda i: (ro + ic(i), cgr)),
        pl.BlockSpec((tm, 512), lambda i: (ro + ic(i), cpu)),
        pl.BlockSpec((POOL_HALO, 512), lambda i: (jnp.maximum(ro16 + ic(i) * r16 - 1, 0), cpu)),
        pl.BlockSpec((POOL_HALO, 512), lambda i: (jnp.minimum(ro16 + (ic(i) + 1) * r16, n16 - 1), cpu)),
        pl.BlockSpec((tm, N_BRANCH * D), lambda i: (ro + ic(i), 0)),
        pl.BlockSpec((tm, ATT_Q), lambda i: (ic(i), 0)),
        pl.BlockSpec((tm, D), lambda i: (ro + ic(i), 0)),
        mod_spec(2), mod_spec(3), mod_spec(4),
        full((1, GLA_DV)), full((4, 128, 128 + 2 * POOL_HALO)), full((4, POOL_GROUP, POOL_GROUP)), full((1, POOL_WIDTH)),
        full((N_BRANCH, BRANCH_WIDTH, D)), full((D, D)), full((1, D)), full((D, ROUTER_W)), full((D, ROUTER_W)),
        full((1, ROUTER_W)),
    ]
    args = [o_f, o_b, u, u, u, u, u, y_att, xin, mods, mods, mods, *wts]
    out_shape = [jax.ShapeDtypeStruct((n_out_rows, D), F32), jax.ShapeDtypeStruct((n_out_rows * TOK_ROWS, LANES), F32),
                 jax.ShapeDtypeStruct((n_out_rows, ROUTER_W), F32)]
    out_specs = [pl.BlockSpec((tm, D), lambda i: (ro + i, 0)), pl.BlockSpec((tm * TOK_ROWS, LANES), lambda i: (ro + i, 0)),
                 pl.BlockSpec((tm, ROUTER_W), lambda i: (ro + i, 0))]
    aliases = {}
    if prev_outs is not None:
        n_in = len(args)
        in_specs += [pl.BlockSpec(memory_space=pl.ANY)] * 3
        args += list(prev_outs)
        aliases = {n_in: 0, n_in + 1: 1, n_in + 2: 2}
    kern = functools.partial(_merge_kernel, n_seq_tiles=n_seq_tiles, fill_tail=fill_tail,
                             T=T, tm=tm, tiles_per_batch=tpb, row_base=mod_row)
    return pl.pallas_call(
        kern,
        grid=(n_grid,),
        in_specs=in_specs,
        out_specs=out_specs,
        out_shape=out_shape,
        input_output_aliases=aliases,
        compiler_params=_cparams(("arbitrary",)),
        name="merge",
    )(*args)


def _router_kernel(lg_ref, out_ref, cnt_ref, carry, pstart, *, tm, bm):
    ph = pl.program_id(0)
    i = pl.program_id(1)
    row = lax.broadcasted_iota(jnp.int32, (LANES, 1), 0)
    neg = -jnp.inf
    x = lg_ref[...].T
    colmax = lambda a: jnp.max(a, axis=0, keepdims=True)
    colmin = lambda a: jnp.min(a, axis=0, keepdims=True)
    colsum = lambda a: jnp.sum(a, axis=0, keepdims=True)

    @pl.when((ph == 0) & (i == 0))
    def _():
        carry[...] = jnp.zeros_like(carry)

    gl = jnp.where(row < N_GROUPS, x, neg)
    gmax = colmax(gl)
    gsum = colsum(jnp.exp(gl - gmax))
    grp = colmin(jnp.where(gl == gmax, row, LANES))
    e_row = row - N_GROUPS
    row_grp = sum((e_row >= EXP_PER_GROUP * k).astype(jnp.int32) for k in range(1, N_GROUPS))
    row_grp = jnp.where(e_row < 0, -1, jnp.where(e_row < N_EXPERTS, row_grp, -1))
    el = jnp.where(row_grp == grp, x, neg)
    emax = colmax(el)
    esum = colsum(jnp.exp(el - emax))
    i1 = colmin(jnp.where(el == emax, row, LANES))
    el2 = jnp.where(row == i1, neg, el)
    m2 = colmax(el2)
    i2 = colmin(jnp.where(el2 == m2, row, LANES))
    p1 = 1.0 / esum
    p2 = jnp.exp(m2 - emax) / esum
    pg = 1.0 / gsum
    w1 = pg * p1 / (p1 + p2)
    w2 = pg * p2 / (p1 + p2)
    oh1 = row == i1
    oh2 = row == i2
    oh = jnp.where(oh1, 1.0, jnp.where(oh2, 1.0, 0.0))

    @pl.when(ph == 0)
    def _():
        carry[...] += jnp.sum(oh, axis=1, keepdims=True)

    @pl.when((ph == 1) & (i == 0))
    def _():
        cnt = carry[...]
        cnt_ref[...] = jnp.broadcast_to(cnt, cnt_ref.shape)
        nb = jnp.floor((cnt + (bm - 1)) * (1.0 / bm))
        hi = jnp.floor(nb * (1.0 / 16.0))
        lo = nb - 16.0 * hi
        r = lax.broadcasted_iota(jnp.int32, (LANES, LANES), 0)
        c = lax.broadcasted_iota(jnp.int32, (LANES, LANES), 1)
        lower = jnp.where(c < r, 1.0, 0.0).astype(BF16)
        hib = jnp.broadcast_to(hi, (LANES, LANES)).astype(BF16)
        lob = jnp.broadcast_to(lo, (LANES, LANES)).astype(BF16)
        pre = 16.0 * _dot(lower, hib) + _dot(lower, lob)
        pstart[...] = pre[:, 0:1] * bm
        carry[...] = jnp.zeros_like(carry)

    @pl.when(ph == 1)
    def _():
        s_i = lax.broadcasted_iota(jnp.int32, (tm, tm), 0)
        t_i = lax.broadcasted_iota(jnp.int32, (tm, tm), 1)
        upper = jnp.where(s_i < t_i, 1.0, 0.0).astype(BF16)
        before = _dot(oh.astype(BF16), upper)
        slot = pstart[...] + carry[...] + before
        d1 = colsum(jnp.where(oh1, slot, 0.0))
        d2 = colsum(jnp.where(oh2, slot, 0.0))
        carry[...] += jnp.sum(oh, axis=1, keepdims=True)
        r8 = lax.broadcasted_iota(jnp.int32, (8, 1), 0)
        top = jnp.where(r8 == 0, d1, jnp.where(r8 == 1, d2, jnp.where(r8 == 2, w1, jnp.where(r8 == 3, w2, 0.0))))
        out_ref[...] = jnp.concatenate([top, jnp.zeros((LANES - 8, tm), F32)], axis=0).T


def _router_call(logits, bm):
    N = logits.shape[0]
    tm = TM_ROUTE
    return pl.pallas_call(
        functools.partial(_router_kernel, tm=tm, bm=bm),
        grid=(2, N // tm),
        in_specs=[pl.BlockSpec((tm, ROUTER_W), lambda p, i: (i, 0))],
        out_specs=[pl.BlockSpec((tm, LANES), lambda p, i: (i * p, 0)), pl.BlockSpec((LANES, LANES), lambda p, i: (0, 0))],
        out_shape=[jax.ShapeDtypeStruct((N, LANES), F32), jax.ShapeDtypeStruct((LANES, LANES), F32)],
        scratch_shapes=[pltpu.VMEM((LANES, 1), F32), pltpu.VMEM((LANES, 1), F32)],
        compiler_params=_cparams(("arbitrary", "arbitrary")),
        name="router",
    )(logits)


def _by_parity(i, fn):
    for p in range(2):
        pl.when(i % 2 == p)(functools.partial(fn, p))


def _dispatch_kernel(idx_hbm, h2_ref, xs_in, xs_hbm, idx_a, idx_b, isem, dsem, *, tm, n_tiles):
    del xs_in
    i = pl.program_id(0)
    idx_s = (idx_a, idx_b)

    def idx_copy(j, slot):
        return pltpu.make_async_copy(idx_hbm.at[j], idx_s[slot], isem.at[slot])

    def row_copy(r, slot):
        return pltpu.make_async_copy(h2_ref.at[r], xs_hbm.at[slot], dsem)

    @pl.when(i == 0)
    def _():
        idx_copy(0, 0).start()

    def step(p):
        @pl.when(i + 1 < n_tiles)
        def _():
            idx_copy(i + 1, 1 - p).start()

        idx_copy(i, p).wait()

        def issue(r0, c):
            for u in range(DMA_UNROLL):
                r = r0 * DMA_UNROLL + u
                for k in range(TOP_K):
                    row_copy(r, idx_s[p][0, TOP_K * r + k]).start()
            return c

        lax.fori_loop(0, tm // DMA_UNROLL, issue, 0)

    _by_parity(i, step)

    def drain(r0, c):
        for _ in range(TOP_K * DMA_UNROLL):
            row_copy(0, 0).wait()
        return c

    lax.fori_loop(0, tm // DMA_UNROLL, drain, 0)


def _dispatch_call(idx, h2, n_slots):
    n_tiles = idx.shape[0]
    tm = TM_DISP
    xs0 = jnp.zeros((n_slots, TOK_ROWS, LANES), F32)
    return pl.pallas_call(
        functools.partial(_dispatch_kernel, tm=tm, n_tiles=n_tiles),
        grid=(n_tiles,),
        in_specs=[pl.BlockSpec(memory_space=pl.ANY),
                  pl.BlockSpec((tm, TOK_ROWS, LANES), lambda i: (i, 0, 0)),
                  pl.BlockSpec(memory_space=pl.ANY)],
        out_specs=pl.BlockSpec(memory_space=pl.ANY),
        out_shape=jax.ShapeDtypeStruct((n_slots, TOK_ROWS, LANES), F32),
        input_output_aliases={2: 0},
        scratch_shapes=[pltpu.SMEM((1, TOP_K * tm), jnp.int32), pltpu.SMEM((1, TOP_K * tm), jnp.int32),
                        pltpu.SemaphoreType.DMA((2,)),
                        pltpu.SemaphoreType.DMA(())],
        compiler_params=pltpu.CompilerParams(dimension_semantics=("arbitrary",), vmem_limit_bytes=VMEM_LIMIT_BYTES,
                                             has_side_effects=True),
        name="dispatch",
    )(idx, h2, xs0)


def _expert_kernel(be_ref, nu_ref, xs_ref, wg_ref, wu_ref, wd_ref, ys_ref, wgb, wub, wdb, *, bm):
    i = pl.program_id(0)

    @pl.when((i == 0) | (be_ref[i] != be_ref[jnp.maximum(i - 1, 0)]))
    def _():
        wgb[...] = wg_ref[...].astype(BF16)
        wub[...] = wu_ref[...].astype(BF16)
        wdb[...] = wd_ref[...].astype(BF16)

    @pl.when(i < nu_ref[0])
    def _():
        part = bm // EXPERT_PARTS
        gu = []
        for c in range(EXPERT_PARTS):
            xb = _load_token_rows(xs_ref, part, c * part * TOK_ROWS).astype(BF16)
            gu.append((_dot(xb, wgb[...]), _dot(xb, wub[...])))
        for c, (g, up) in enumerate(gu):
            hmid = (_silu(g) * up).astype(BF16)
            _store_token_rows(ys_ref, _dot(hmid, wdb[...]), c * part * TOK_ROWS)

    @pl.when(i >= nu_ref[0])
    def _():
        ys_ref[...] = jnp.zeros_like(ys_ref)


def _expert_call(blk_expert, nused, xs, wg, wu, wd, l):
    bm = BM_MOE
    D = D_MODEL
    nblk = blk_expert.shape[0]
    w_in_spec = pl.BlockSpec((None, None, D, D_EXPERT), lambda i, be, nu: (l, be[i], 0, 0))
    grid_spec = pltpu.PrefetchScalarGridSpec(
        num_scalar_prefetch=2,
        grid=(nblk,),
        in_specs=[
            pl.BlockSpec((bm * TOK_ROWS, LANES), lambda i, be, nu: (jnp.minimum(i, nu[0] - 1), 0)),
            w_in_spec, w_in_spec,
            pl.BlockSpec((None, None, D_EXPERT, D), lambda i, be, nu: (l, be[i], 0, 0)),
        ],
        out_specs=pl.BlockSpec((bm * TOK_ROWS, LANES), lambda i, be, nu: (i, 0)),
        scratch_shapes=[pltpu.VMEM((D, D_EXPERT), BF16), pltpu.VMEM((D, D_EXPERT), BF16),
                        pltpu.VMEM((D_EXPERT, D), BF16)],
    )
    return pl.pallas_call(
        functools.partial(_expert_kernel, bm=bm),
        grid_spec=grid_spec,
        out_shape=jax.ShapeDtypeStruct(xs.shape, F32),
        compiler_params=_cparams(("arbitrary",)),
        name="experts",
    )(blk_expert, nused, xs, wg, wu, wd)


def _combine_kernel(idx_hbm, ys_hbm, x_ref, r_ref, g2_ref, fg_ref, o_ref, idx_a, idx_b, isem, ybuf, gsem, *,
                    tm, n_tiles, n_lat_tiles, tiles_per_batch, ctx_row, final):
    i = pl.program_id(0)
    idx_s = (idx_a, idx_b)

    def idx_copy(j, slot):
        return pltpu.make_async_copy(idx_hbm.at[j], idx_s[slot], isem.at[slot])

    def row_copy(slot, r, k, src):
        rows = pl.ds(pl.multiple_of((k * tm + r) * TOK_ROWS, TOK_ROWS), TOK_ROWS)
        return pltpu.make_async_copy(ys_hbm.at[src], ybuf.at[slot, rows, :], gsem.at[slot])

    def start_gather(slot):
        def issue(r0, c):
            for u in range(DMA_UNROLL):
                r = r0 * DMA_UNROLL + u
                for k in range(TOP_K):
                    row_copy(slot, r, k, idx_s[slot][0, TOP_K * r + k]).start()
            return c
        lax.fori_loop(0, tm // DMA_UNROLL, issue, 0)

    def wait_gather(slot):
        def drain(r0, c):
            for _ in range(TOP_K * DMA_UNROLL):
                row_copy(slot, 0, 0, 0).wait()
            return c
        lax.fori_loop(0, tm // DMA_UNROLL, drain, 0)

    @pl.when(i == 0)
    def _():
        idx_copy(0, 0).start()
        idx_copy(0, 0).wait()
        start_gather(0)
        if n_tiles > 1:
            idx_copy(1, 1).start()

    def step(p):
        @pl.when(i + 1 < n_tiles)
        def _():
            idx_copy(i + 1, 1 - p).wait()
            start_gather(1 - p)

        @pl.when(i + 2 < n_tiles)
        def _():
            idx_copy(i + 2, p).start()

        wait_gather(p)
        yb = ybuf.at[p]
        y0 = _load_token_rows(yb, tm, 0)
        y1 = _load_token_rows(yb, tm, tm * TOK_ROWS)
        m = r_ref[:, 2:3] * y0 + r_ref[:, 3:4] * y1
        b = jnp.where(i < n_lat_tiles, i // tiles_per_batch, ctx_row)
        xn = x_ref[...] + g2_ref[pl.ds(b, 1), :] * m
        if final:
            ms = jnp.mean(xn * xn, axis=-1, keepdims=True)
            xn = xn * lax.rsqrt(ms + NORM_EPS) * fg_ref[...]
        o_ref[...] = xn

    _by_parity(i, step)


def _combine_call(idx, ys, x_mid, route, mods, l, final_g, n_lat_rows, rows_per_batch, ctx_row, final):
    tm = TM_COMB
    D = D_MODEL
    R = mods.shape[2]
    n_rows = x_mid.shape[0]
    n_tiles = n_rows // tm
    kern = functools.partial(_combine_kernel, tm=tm, n_tiles=n_tiles, n_lat_tiles=n_lat_rows // tm,
                             tiles_per_batch=rows_per_batch // tm, ctx_row=ctx_row, final=final)
    return pl.pallas_call(
        kern,
        grid=(n_tiles,),
        in_specs=[
            pl.BlockSpec(memory_space=pl.ANY),
            pl.BlockSpec(memory_space=pl.ANY),
            pl.BlockSpec((tm, D), lambda i: (i, 0)),
            pl.BlockSpec((tm, LANES), lambda i: (i, 0)),
            pl.BlockSpec((None, None, R, D), lambda i: (l, 5, 0, 0)),
            pl.BlockSpec((1, D), lambda i: (0, 0)),
        ],
        out_specs=pl.BlockSpec((tm, D), lambda i: (i, 0)),
        out_shape=jax.ShapeDtypeStruct((n_rows, D), F32),
        scratch_shapes=[pltpu.SMEM((1, TOP_K * tm), jnp.int32), pltpu.SMEM((1, TOP_K * tm), jnp.int32),
                        pltpu.SemaphoreType.DMA((2,)),
                        pltpu.VMEM((2, TOP_K * tm * TOK_ROWS, LANES), F32), pltpu.SemaphoreType.DMA((2,))],
        compiler_params=_cparams(("arbitrary",)),
        name="combine",
    )(idx, ys, x_mid, route, mods, final_g.reshape(1, D))


def _permute_w_in(w_in):
    gq, gk, gv, gr, glf, glb, aq, ak, av, pu, gt = jnp.split(w_in, np.cumsum(IN_SIZES)[:-1].tolist(), axis=-1)
    pad = jnp.zeros(w_in.shape[:-1] + (U_WIDTH - sum(IN_SIZES),), w_in.dtype)
    return jnp.concatenate([gt, gq, gk, gv, gr, aq, pu, ak, av, glf, glb, pad], axis=-1).astype(BF16)


def _rope_tables(T):
    rows = T // GRID_W
    row = np.repeat(np.arange(rows), GRID_W).astype(np.float32)
    col = np.tile(np.arange(GRID_W), rows).astype(np.float32)
    inv = jnp.asarray(ROPE_THETA, F32) ** (-jnp.arange(0, ROPE_AXIS_DIM, 2, dtype=F32) / ROPE_AXIS_DIM)
    ang_r = jnp.asarray(row)[:, None] * inv
    ang_c = jnp.asarray(col)[:, None] * inv
    zero = jnp.zeros_like(ang_r)
    cos = jnp.concatenate([jnp.cos(ang_r)] * 2 + [jnp.cos(ang_c)] * 2, axis=1)
    s1 = jnp.concatenate([-jnp.sin(ang_r), zero, -jnp.sin(ang_c), zero], axis=1)
    s2 = jnp.concatenate([zero, jnp.sin(ang_r), zero, jnp.sin(ang_c)], axis=1)
    return tuple(jnp.concatenate([t, t], axis=1) for t in (cos, s1, s2))


def _pool_bands():
    i = np.arange(128)[:, None]
    j = np.arange(128 + 2 * POOL_HALO)[None, :]
    bands = [((j >= i + POOL_HALO - w // 2) & (j < i + POOL_HALO + w // 2)).astype(np.float32) for w in POOL_WINDOWS]
    return jnp.asarray(np.stack(bands), BF16)


def _block_table(counts, n_tok, bm):
    cnt = counts[N_GROUPS:N_GROUPS + N_EXPERTS, 0].astype(jnp.int32)
    pad_end = jnp.cumsum((cnt + bm - 1) // bm * bm)
    nblk = -(-(n_tok * TOP_K + N_EXPERTS * (bm - 1)) // bm)
    blk_start = jnp.arange(nblk, dtype=jnp.int32) * bm
    blk_expert = jnp.minimum(jnp.sum((pad_end[None, :] <= blk_start[:, None]).astype(jnp.int32), axis=1), N_EXPERTS - 1)
    nused = (pad_end[-1] // bm).astype(jnp.int32).reshape(1)
    return blk_expert.astype(jnp.int32), nused, nblk


def _slot_tiles(route, tm):
    n = route.shape[0]
    return route[:, :TOP_K].astype(jnp.int32).reshape(n // tm, 1, TOP_K * tm)


def kernel(x, c, ctx, c_ctx, w_mod, b_mod, norm1_g, norm2_g, w_in, gla_a_up_f, gla_a_bias_f, gla_a_up_b, gla_a_bias_b, gla_norm_g, att_qn_g, att_kn_g, pool_w, pool_scale, w_branch, w_out, moe_w_group, moe_b_group, moe_w_expert, moe_b_expert, moe_w_gate, moe_w_up, moe_w_down, final_g):
    B, T, D = x.shape
    C = ctx.shape[1]
    L = w_mod.shape[0]
    n_lat, n_ctx = B * T, B * C
    MOD_ROWS = 16
    assert D == D_MODEL and B < MOD_ROWS and T % TQ_ATT == 0 and C % GLA_CHUNK == 0

    s_in = jnp.concatenate([c, c_ctx[None], jnp.zeros((MOD_ROWS - B - 1, D), F32)], axis=0)
    mods = _mod_call(s_in, w_mod, b_mod)
    w_perm = _permute_w_in(w_in)
    tabs = _rope_tables(T)
    bands = _pool_bands()
    zero_state = jnp.zeros((B, GLA_V, GLA_QK), F32)

    xall = jnp.concatenate([x.reshape(n_lat, D), ctx.reshape(n_ctx, D)], axis=0)
    out = None
    for l in range(L):
        want_ctx = l < L - 1
        n_rows = xall.shape[0]
        u = _inproj_call(xall, mods, l, norm1_g[l], w_perm[l], n_lat, T, B)

        up = jnp.zeros((128, 2 * GLA_QK), F32)
        up = up.at[:GLA_RANK, :GLA_QK].set(gla_a_up_f[l]).at[GLA_RANK:2 * GLA_RANK, GLA_QK:].set(gla_a_up_b[l]).astype(BF16)
        bias = jnp.concatenate([gla_a_bias_f[l], gla_a_bias_b[l]]).reshape(1, 2 * GLA_QK)
        ofc, obc, sfc, sbc = _gla_call(u, n_lat, B, C, zero_state, zero_state, up, bias)
        of, ob, _, _ = _gla_call(u, 0, B, T, sfc, sbc, up, bias)

        qg = jnp.tile(att_qn_g[l], 2).reshape(1, LANES)
        kg = jnp.tile(att_kn_g[l], 2).reshape(1, LANES)
        ya = _att_call(u, 0, B, T, T, C, n_lat, tabs, qg, kg)

        wr = jnp.zeros((D, ROUTER_W), F32).at[:, :N_GROUPS].set(moe_w_group[l]).at[:, N_GROUPS:N_GROUPS + N_EXPERTS].set(moe_w_expert[l])
        wrh, wrl = _split_bf16(wr)
        br = jnp.zeros((1, ROUTER_W), F32).at[0, :N_GROUPS].set(moe_b_group[l]).at[0, N_GROUPS:N_GROUPS + N_EXPERTS].set(moe_b_expert[l])
        wts = (gla_norm_g[l].reshape(1, GLA_DV), bands, pool_w[l].astype(BF16), pool_scale[l].reshape(1, POOL_WIDTH),
               w_branch[l].astype(BF16), w_out[l].astype(BF16), norm2_g[l].reshape(1, D), wrh, wrl, br)
        n_tok = n_rows if want_ctx else n_lat
        outs = _merge_call(of, ob, u, ya, xall, 0, B, T, mods, l, None, wts, n_tok, None)
        if want_ctx:
            yac = _att_call(u, n_lat, B, C, 0, C, n_lat, None, qg, kg)
            outs = _merge_call(ofc, obc, u, yac, xall, n_lat, B, C, mods, l, B, wts, n_tok, outs)
        x_mid, h2, logits = outs

        route, counts = _router_call(logits, BM_MOE)
        blk_expert, nused, nblk = _block_table(counts, n_tok, BM_MOE)
        n_slots = nblk * BM_MOE
        xs = _dispatch_call(_slot_tiles(route, TM_DISP), h2.reshape(n_tok, TOK_ROWS, LANES), n_slots)
        ys = _expert_call(blk_expert, nused, xs.reshape(n_slots * TOK_ROWS, LANES), moe_w_gate, moe_w_up, moe_w_down, l)
        xall = _combine_call(_slot_tiles(route, TM_COMB), ys.reshape(n_slots, TOK_ROWS, LANES), x_mid, route, mods, l,
                             final_g, n_lat, T, B, final=not want_ctx)
    return xall[:n_lat].reshape(B, T, D)
```
